```python
import jax, jax.numpy as jnp
from jax import lax
import numpy as np

D_MODEL = 1024
BATCH = 8
SEQ = 8192
DEPTH = 4

N_META = 16
MLSTM_HEADS = 4
MLSTM_DQK = 128
MLSTM_DV = 256
MLSTM_CHUNK = 64
QK_CONV_WIDTH = 4
GATE_SOFTCAP = 15.0
SB_HEADS = 4
SB_DH = 128
SB_BLOCK = 128
PAD_FRONT = SB_BLOCK - N_META
CONV_WIDTH = 31
FFN_HIDDEN = -(-8 * D_MODEL // (3 * 256)) * 256
MQK = MLSTM_HEADS * MLSTM_DQK
MV = MLSTM_HEADS * MLSTM_DV
SBW = SB_HEADS * SB_DH
IN_SIZES = (2 * MQK, MV, MV, 2 * MLSTM_HEADS, SBW, SBW, SBW)
IN_WIDTH = sum(IN_SIZES)
MIX_WIDTH = MV + SBW
N_EVEN = (DEPTH + 1) // 2
N_ODD = DEPTH // 2
NEG = -1e30
EPS = 1e-6

kernel_name = 'hybrid_mlstm_stickbreaking_conformer'


def rms_norm(x, g):
    xf = x.astype(jnp.float32)
    y = xf * lax.rsqrt(jnp.mean(xf * xf, axis=-1, keepdims=True) + EPS)
    return (y * g.astype(jnp.float32)).astype(x.dtype)


def layer_norm(x, g, b):
    xf = x.astype(jnp.float32)
    mu = jnp.mean(xf, axis=-1, keepdims=True)
    var = jnp.mean(jnp.square(xf - mu), axis=-1, keepdims=True)
    y = (xf - mu) * lax.rsqrt(var + EPS)
    return (y * g.astype(jnp.float32) + b.astype(jnp.float32)).astype(x.dtype)


def causal_depthwise_conv(x, w, b):
    k = w.shape[0]
    y = lax.conv_general_dilated(x, w[:, None, :].astype(x.dtype), (1,), [(k - 1, 0)],
                                 dimension_numbers=('NWC', 'WIO', 'NWC'),
                                 feature_group_count=x.shape[-1])
    return y + b.astype(x.dtype)


def to_chunks(a):
    b, h, t = a.shape[:3]
    a = a.reshape(b, h, t // MLSTM_CHUNK, MLSTM_CHUNK, *a.shape[3:])
    return jnp.moveaxis(a, 2, 0)


def mlstm_chunkwise(q, k, v, log_i, log_f):
    b, h, t, dk = q.shape
    dv = v.shape[-1]
    tril = jnp.tril(jnp.ones((MLSTM_CHUNK, MLSTM_CHUNK), dtype=bool))

    def step(carry, xs):
        c_st, n_st, m_st = carry
        qc, kc, vc, li, lf = xs
        bcum = jnp.cumsum(lf, axis=-1)
        g = bcum[..., -1]
        dmat = bcum[..., :, None] - bcum[..., None, :] + li[..., None, :]
        dmat = jnp.where(tril, dmat, NEG)
        inter = bcum + m_st[..., None]
        m_t = jnp.maximum(inter, jnp.max(dmat, axis=-1))
        w_intra = jnp.exp(dmat - m_t[..., None])
        w_inter = jnp.exp(inter - m_t)
        s = jnp.einsum('bhtd,bhsd->bhts', qc, kc) * w_intra
        num = jnp.einsum('bhts,bhsv->bhtv', s, vc) + w_inter[..., None] * jnp.einsum('bhtd,bhdv->bhtv', qc, c_st)
        den = jnp.sum(s, axis=-1) + w_inter * jnp.einsum('bhtd,bhd->bht', qc, n_st)
        h_out = num / jnp.maximum(jnp.abs(den), jnp.exp(-m_t))[..., None]
        a = g[..., None] - bcum + li
        m_new = jnp.maximum(g + m_st, jnp.max(a, axis=-1))
        wa = jnp.exp(a - m_new[..., None])
        wc = jnp.exp(g + m_st - m_new)
        c_new = wc[..., None, None] * c_st + jnp.einsum('bhs,bhsd,bhsv->bhdv', wa, kc, vc)
        n_new = wc[..., None] * n_st + jnp.einsum('bhs,bhsd->bhd', wa, kc)
        return (c_new, n_new, m_new), h_out

    init = (jnp.zeros((b, h, dk, dv), jnp.float32), jnp.zeros((b, h, dk), jnp.float32),
            jnp.zeros((b, h), jnp.float32))
    xs = (to_chunks(q), to_chunks(k), to_chunks(v), to_chunks(log_i), to_chunks(log_f))
    _, hs = lax.scan(step, init, xs)
    return jnp.moveaxis(hs, 0, 2).reshape(b, h, t, dv)


def stick_breaking(q, k, v, valid):
    b, h, t, d = q.shape
    nb = t // SB_BLOCK
    scale = d ** -0.5
    r = jnp.arange(SB_BLOCK)
    rev_in = (r[:, None] >= r[None, :]).astype(jnp.float32)
    outs = []
    for i in range(nb):
        nk = i + 1
        end = nk * SB_BLOCK
        qi = q[:, :, i * SB_BLOCK:end]
        kb = k[:, :, :end].reshape(b, h, nk, SB_BLOCK, d)
        vb = v[:, :, :end].reshape(b, h, nk, SB_BLOCK, d)
        z = jnp.einsum('bhtd,bhnsd->bhtns', qi, kb).astype(jnp.float32) * scale
        t_idx = i * SB_BLOCK + r
        s_idx = jnp.arange(end).reshape(nk, SB_BLOCK)
        mask = (s_idx[None] < t_idx[:, None, None]) & valid[:end].reshape(nk, SB_BLOCK)[None]
        log1m = jnp.where(mask, -jax.nn.softplus(z), 0.0)
        within = jnp.einsum('bhtns,su->bhtnu', log1m, rev_in, precision=lax.Precision.HIGHEST)
        n_r = jnp.arange(nk)
        rev_blk = (n_r[:, None] > n_r[None, :]).astype(jnp.float32)
        across = jnp.einsum('bhtn,nm->bhtm', within[..., 0], rev_blk, precision=lax.Precision.HIGHEST)
        log_w = jnp.where(mask, z + within + across[..., None], NEG)
        w = jnp.exp(log_w)
        outs.append(jnp.einsum('bhtns,bhnsd->bhtd', w, vb.astype(jnp.float32)))
    return jnp.concatenate(outs, axis=2)


def mlstm_sb_mixer(u, w_in, qk_conv_w, qk_conv_b, gate_b, hnorm_g, w_out):
    b, t, _ = u.shape
    tp = t + PAD_FRONT
    up = jnp.pad(u, ((0, 0), (PAD_FRONT, 0), (0, 0)))
    valid = jnp.arange(tp) >= PAD_FRONT
    proj = up @ w_in
    qk_m, v_m, o_m, gates, q_s, k_s, v_s = jnp.split(proj, np.cumsum(IN_SIZES)[:-1].tolist(), axis=-1)
    qk_m = jax.nn.silu(causal_depthwise_conv(qk_m, qk_conv_w, qk_conv_b))
    q_m, k_m = jnp.split(qk_m, 2, axis=-1)
    gates = gates.astype(jnp.float32) + gate_b.astype(jnp.float32)
    gates = GATE_SOFTCAP * jnp.tanh(gates / GATE_SOFTCAP)
    log_i = jnp.where(valid[:, None], gates[..., :MLSTM_HEADS], NEG)
    log_f = jnp.where(valid[:, None], jax.nn.log_sigmoid(gates[..., MLSTM_HEADS:]), 0.0)

    def heads(a, nh):
        return a.reshape(b, tp, nh, -1).transpose(0, 2, 1, 3)

    h_m = mlstm_chunkwise(heads(q_m, MLSTM_HEADS) * MLSTM_DQK ** -0.5, heads(k_m, MLSTM_HEADS),
                          heads(v_m, MLSTM_HEADS), log_i.transpose(0, 2, 1), log_f.transpose(0, 2, 1))
    h_m = rms_norm(h_m.transpose(0, 2, 1, 3), hnorm_g.reshape(MLSTM_HEADS, MLSTM_DV)).reshape(b, tp, MV)
    h_m = h_m * jax.nn.sigmoid(o_m.astype(jnp.float32))
    h_s = stick_breaking(heads(q_s, SB_HEADS), heads(k_s, SB_HEADS), heads(v_s, SB_HEADS), valid)
    h_s = h_s.transpose(0, 2, 1, 3).reshape(b, tp, SBW)
    mixed = jnp.concatenate([h_m, h_s], axis=-1)[:, PAD_FRONT:].astype(u.dtype)
    return mixed @ w_out


def conformer_conv(u, w_pw1, b_pw1, w_dw, b_dw, ln_g, ln_b, w_pw2, b_pw2):
    a, gate = jnp.split(u @ w_pw1 + b_pw1, 2, axis=-1)
    y = a * jax.nn.sigmoid(gate)
    y = causal_depthwise_conv(y, w_dw, b_dw)
    y = jax.nn.silu(layer_norm(y, ln_g, ln_b))
    return y @ w_pw2 + b_pw2


def swiglu(u, w_gate, w_up, w_down):
    return (jax.nn.silu(u @ w_gate) * (u @ w_up)) @ w_down


def setup_inputs(seed: int = 0) -> dict:
    key = jax.random.key(seed)
    ks = jax.random.split(key, 24)
    f32 = jnp.float32
    nrm = lambda k, shape, s: jax.random.normal(k, shape, f32) * s
    gate_noise = nrm(ks[5], (N_EVEN, 2 * MLSTM_HEADS), 0.1)
    gate_center = jnp.concatenate([jnp.full((MLSTM_HEADS,), -2.0, f32), jnp.full((MLSTM_HEADS,), 3.0, f32)])
    return {
        'x': nrm(ks[0], (BATCH, SEQ, D_MODEL), 1.0),
        'meta': nrm(ks[1], (N_META, D_MODEL), 1.0),
        'norm_g': 1.0 + nrm(ks[2], (DEPTH, 4, D_MODEL), 0.02),
        'mix_w_in': nrm(ks[3], (N_EVEN, D_MODEL, IN_WIDTH), D_MODEL ** -0.5),
        'mix_qk_conv_w': nrm(ks[4], (N_EVEN, QK_CONV_WIDTH, 2 * MQK), QK_CONV_WIDTH ** -0.5),
        'mix_qk_conv_b': nrm(ks[6], (N_EVEN, 2 * MQK), 0.02),
        'mix_gate_b': gate_center + gate_noise,
        'mix_hnorm_g': 1.0 + nrm(ks[7], (N_EVEN, MV), 0.02),
        'mix_w_out': nrm(ks[8], (N_EVEN, MIX_WIDTH, D_MODEL), MIX_WIDTH ** -0.5),
        'conv_w_pw1': nrm(ks[9], (N_ODD, D_MODEL, 2 * D_MODEL), D_MODEL ** -0.5),
        'conv_b_pw1': nrm(ks[10], (N_ODD, 2 * D_MODEL), 0.02),
        'conv_w_dw': nrm(ks[11], (N_ODD, CONV_WIDTH, D_MODEL), CONV_WIDTH ** -0.5),
        'conv_b_dw': nrm(ks[12], (N_ODD, D_MODEL), 0.02),
        'conv_ln_g': 1.0 + nrm(ks[13], (N_ODD, D_MODEL), 0.02),
        'conv_ln_b': nrm(ks[14], (N_ODD, D_MODEL), 0.02),
        'conv_w_pw2': nrm(ks[15], (N_ODD, D_MODEL, D_MODEL), D_MODEL ** -0.5),
        'conv_b_pw2': nrm(ks[16], (N_ODD, D_MODEL), 0.02),
        'ffn_w_gate': nrm(ks[17], (DEPTH, D_MODEL, FFN_HIDDEN), D_MODEL ** -0.5),
        'ffn_w_up': nrm(ks[18], (DEPTH, D_MODEL, FFN_HIDDEN), D_MODEL ** -0.5),
        'ffn_w_down': nrm(ks[19], (DEPTH, FFN_HIDDEN, D_MODEL), FFN_HIDDEN ** -0.5),
    }


def reference(x, meta, norm_g, mix_w_in, mix_qk_conv_w, mix_qk_conv_b, mix_gate_b, mix_hnorm_g,
              mix_w_out, conv_w_pw1, conv_b_pw1, conv_w_dw, conv_b_dw, conv_ln_g, conv_ln_b,
              conv_w_pw2, conv_b_pw2, ffn_w_gate, ffn_w_up, ffn_w_down):
    b = x.shape[0]
    h = jnp.concatenate([jnp.broadcast_to(meta[None].astype(x.dtype), (b, N_META, D_MODEL)), x], axis=1)
    for layer in range(DEPTH):
        g = norm_g[layer]
        u = rms_norm(h, g[0])
        i = layer // 2
        if layer % 2 == 0:
            y = mlstm_sb_mixer(u, mix_w_in[i], mix_qk_conv_w[i], mix_qk_conv_b[i], mix_gate_b[i],
                               mix_hnorm_g[i], mix_w_out[i])
        else:
            y = conformer_conv(u, conv_w_pw1[i], conv_b_pw1[i], conv_w_dw[i], conv_b_dw[i],
                               conv_ln_g[i], conv_ln_b[i], conv_w_pw2[i], conv_b_pw2[i])
        h = h + rms_norm(y, g[1])
        f = swiglu(rms_norm(h, g[2]), ffn_w_gate[layer], ffn_w_up[layer], ffn_w_down[layer])
        h = h + rms_norm(f, g[3])
    return h[:, N_META:]
```

```python
import functools

import jax
import jax.numpy as jnp
from jax import lax
from jax.experimental import pallas as pl
from jax.experimental.pallas import tpu as pltpu

N_META = 16
MLSTM_HEADS = 4
MLSTM_DQK = 128
MLSTM_DV = 256
QK_CONV_WIDTH = 4
GATE_SOFTCAP = 15.0
SB_HEADS = 4
SB_DH = 128
SB_BLOCK = 128
PAD_FRONT = SB_BLOCK - N_META
CONV_WIDTH = 31
MQK = MLSTM_HEADS * MLSTM_DQK
MV = MLSTM_HEADS * MLSTM_DV
SBW = SB_HEADS * SB_DH
NEG = -1e30
EPS = 1e-6

LANES = 128
SUBLANES = 8
V7X_VMEM_BYTES = 64 * 1024 * 1024
F32_EXP_UNDERFLOW = -104.0

MLSTM_CHUNK = 128
CONV_HALO = 32
GATE_LANES = LANES

BF16 = jnp.bfloat16
F32 = jnp.float32


def _vmem_limit(block_bytes, scratch_bytes=0, temp_bytes=0):
    est = 2 * block_bytes + scratch_bytes + temp_bytes + (4 << 20)
    return int(min(max(est, 16 << 20), V7X_VMEM_BYTES - (6 << 20)))


def _nbytes(shape, dtype):
    n = 1
    for s in shape:
        n *= s
    return n * jnp.dtype(dtype).itemsize


def _seq_tile(t):
    best = SB_BLOCK
    for cand in range(SB_BLOCK, 1024 + 1, SB_BLOCK):
        if t % cand == 0:
            best = cand
    return best


def _rms(x, g):
    return x * lax.rsqrt(jnp.mean(x * x, axis=-1, keepdims=True) + EPS) * g


def _softplus(x):
    return jnp.maximum(x, 0.0) + jnp.log(1.0 + jnp.exp(-jnp.abs(x)))


def _split3(x):
    hi = x.astype(BF16)
    r = x - hi.astype(F32)
    mid = r.astype(BF16)
    lo = (r - mid.astype(F32)).astype(BF16)
    return hi, mid, lo


def _dot(a, b):
    return jnp.dot(a, b, preferred_element_type=F32)


def _dot_nt(a, b):
    return lax.dot_general(a, b, (((1,), (1,)), ((), ())), preferred_element_type=F32)


def _dot_tn(a, b):
    return lax.dot_general(a, b, (((0,), (0,)), ((), ())), preferred_element_type=F32)


def _in_proj_kernel(h_ref, g_ref, wqk_ref, wv_ref, wo_ref, wg_ref, wsq_ref, wsk_ref, wsv_ref,
                    cw_ref, cb_ref, gb_ref,
                    qm_ref, km_ref, vm_ref, om_ref, gt_ref, qs_ref, ks_ref, vs_ref, conv_scr):
    t = pl.program_id(1)
    tm = h_ref.shape[1]
    u = _rms(h_ref[0], g_ref[...])
    row = t * tm + lax.broadcasted_iota(jnp.int32, (tm, 1), 0)
    valid = row >= PAD_FRONT
    ub = jnp.where(valid, u, 0.0).astype(BF16)

    @pl.when(t == 0)
    def _():
        conv_scr[0:SUBLANES, :] = jnp.zeros((SUBLANES, 2 * MQK), F32)

    conv_scr[SUBLANES:SUBLANES + tm, :] = _dot(ub, wqk_ref[...])
    acc = cb_ref[...]
    for j in range(QK_CONV_WIDTH):
        shift = SUBLANES - (QK_CONV_WIDTH - 1) + j
        acc = acc + cw_ref[j:j + 1, :] * conv_scr[pl.ds(shift, tm), :]
    conv_scr[0:SUBLANES, :] = conv_scr[tm:tm + SUBLANES, :]
    qk = acc * jax.nn.sigmoid(acc)
    qm_ref[0] = (qk[:, :MQK] * MLSTM_DQK ** -0.5).astype(BF16)
    km_ref[0] = qk[:, MQK:].astype(BF16)

    vm_ref[0] = _dot(ub, wv_ref[...]).astype(BF16)
    om_ref[0] = _dot(ub, wo_ref[...])
    qs_ref[0] = _dot(ub, wsq_ref[...]).astype(BF16)
    ks_ref[0] = _dot(ub, wsk_ref[...]).astype(BF16)
    vs_ref[0] = _dot(ub, wsv_ref[...]).astype(BF16)

    gt = _dot(ub, wg_ref[...]) + gb_ref[...]
    gt = GATE_SOFTCAP * jnp.tanh(gt / GATE_SOFTCAP)
    lane = lax.broadcasted_iota(jnp.int32, (1, GATE_LANES), 1)
    log_i = jnp.where(valid, gt, NEG)
    log_f = jnp.where(valid, -_softplus(-gt), 0.0)
    gt_ref[0] = jnp.where(lane < MLSTM_HEADS, log_i, jnp.where(lane < 2 * MLSTM_HEADS, log_f, 0.0))


def _in_proj(h, g0, w_in, conv_w, conv_b, gate_b):
    b, t, d = h.shape
    tm = _seq_tile(t)
    o = [0, 2 * MQK, 2 * MQK + MV, 2 * MQK + 2 * MV, 2 * MQK + 2 * MV + 2 * MLSTM_HEADS]
    o += [o[-1] + SBW, o[-1] + 2 * SBW, o[-1] + 3 * SBW]
    wb = w_in.astype(BF16)
    wqk, wv, wo = wb[:, o[0]:o[1]], wb[:, o[1]:o[2]], wb[:, o[2]:o[3]]
    wg = jnp.pad(wb[:, o[3]:o[4]], ((0, 0), (0, GATE_LANES - 2 * MLSTM_HEADS)))
    wsq, wsk, wsv = wb[:, o[4]:o[5]], wb[:, o[5]:o[6]], wb[:, o[6]:o[7]]
    gb = jnp.pad(gate_b.astype(F32), (0, GATE_LANES - 2 * MLSTM_HEADS)).reshape(1, GATE_LANES)

    def full(a):
        return pl.BlockSpec(a.shape, lambda i, j: (0,) * a.ndim)

    def rows(width):
        return pl.BlockSpec((1, tm, width), lambda i, j: (i, j, 0))

    ins = [h, g0.reshape(1, d), wqk, wv, wo, wg, wsq, wsk, wsv, conv_w, conv_b.reshape(1, -1), gb]
    out_widths = [(MQK, BF16), (MQK, BF16), (MV, BF16), (MV, F32), (GATE_LANES, F32),
                  (SBW, BF16), (SBW, BF16), (SBW, BF16)]
    blk = _nbytes((tm, d), F32) + sum(_nbytes(a.shape, a.dtype) for a in ins[1:])
    blk += sum(_nbytes((tm, w), dt) for w, dt in out_widths)
    scr = _nbytes((tm + 2 * SUBLANES, 2 * MQK), F32)
    return pl.pallas_call(
        _in_proj_kernel,
        grid=(b, t // tm),
        in_specs=[rows(d)] + [full(a) for a in ins[1:]],
        out_specs=[rows(w) for w, _ in out_widths],
        out_shape=[jax.ShapeDtypeStruct((b, t, w), dt) for w, dt in out_widths],
        scratch_shapes=[pltpu.VMEM((tm + 2 * SUBLANES, 2 * MQK), F32)],
        compiler_params=pltpu.CompilerParams(
            dimension_semantics=("arbitrary", "arbitrary"),
            vmem_limit_bytes=_vmem_limit(blk, scr, 4 * _nbytes((tm, 2 * MQK), F32))),
        name="mixer_in_proj",
    )(*ins)


def _mlstm_kernel(q_ref, k_ref, v_ref, o_ref, gt_ref, hg_ref, out_ref, c_scr, n_scr, m_scr):
    n_heads, dk, dv, ln = MLSTM_HEADS, MLSTM_DQK, MLSTM_DV, MLSTM_CHUNK

    @pl.when(pl.program_id(1) == 0)
    def _():
        c_scr[...] = jnp.zeros(c_scr.shape, F32)
        n_scr[...] = jnp.zeros(n_scr.shape, F32)
        m_scr[...] = jnp.zeros(m_scr.shape, F32)

    gates = gt_ref[0]
    ri = lax.broadcasted_iota(jnp.int32, (ln, ln), 0)
    ci = lax.broadcasted_iota(jnp.int32, (ln, ln), 1)
    tril = ci <= ri
    ltri = tril.astype(BF16)
    g_hi, g_mid, g_lo = _split3(gates)
    csum = _dot(ltri, g_hi) + _dot(ltri, g_mid) + _dot(ltri, g_lo)
    gates_t = gates.T
    csum_t = csum.T

    for hd in range(n_heads):
        fcol = n_heads + hd
        bcol = csum[:, fcol:fcol + 1]
        brow = csum_t[fcol:fcol + 1, :]
        li_row = gates_t[hd:hd + 1, :]
        li_col = gates[:, hd:hd + 1]
        m_st = m_scr[hd:hd + 1, 0:1]
        g_tot = csum[ln - 1:ln, fcol:fcol + 1]

        dmat = jnp.where(tril, bcol - brow + li_row, NEG)
        inter = bcol + m_st
        m_t = jnp.maximum(inter, jnp.max(dmat, axis=-1, keepdims=True))
        w_intra = jnp.exp(dmat - m_t)
        w_inter = jnp.exp(inter - m_t)

        q = q_ref[0, :, hd * dk:(hd + 1) * dk]
        k = k_ref[0, :, hd * dk:(hd + 1) * dk]
        v = v_ref[0, :, hd * dv:(hd + 1) * dv]
        s = _dot_nt(q, k) * w_intra
        c_st = c_scr[hd]
        n_st = n_scr[hd:hd + 1, :]
        num = _dot(s.astype(BF16), v) + w_inter * _dot(q, c_st.astype(BF16))
        den = (jnp.sum(s, axis=-1, keepdims=True)
               + w_inter * jnp.sum(q.astype(F32) * n_st, axis=-1, keepdims=True))
        h_out = num / jnp.maximum(jnp.abs(den), jnp.exp(-m_t))

        a = g_tot - bcol + li_col
        m_new = jnp.maximum(g_tot + m_st, jnp.max(a, axis=0, keepdims=True))
        wa = jnp.exp(a - m_new)
        wc = jnp.exp(g_tot + m_st - m_new)
        kw = k.astype(F32) * wa
        c_scr[hd] = wc * c_st + _dot_tn(kw.astype(BF16), v)
        n_scr[hd:hd + 1, :] = wc * n_st + jnp.sum(kw, axis=0, keepdims=True)
        m_scr[hd:hd + 1, :] = jnp.broadcast_to(m_new, (1, LANES))

        hn = _rms(h_out, hg_ref[:, hd * dv:(hd + 1) * dv])
        gate = jax.nn.sigmoid(o_ref[0, :, hd * dv:(hd + 1) * dv])
        out_ref[0, :, hd * dv:(hd + 1) * dv] = (hn * gate).astype(BF16)


def _mlstm(qm, km, vm, om, gt, hnorm_g):
    b, t, _ = qm.shape
    ln = MLSTM_CHUNK

    def rows(width):
        return pl.BlockSpec((1, ln, width), lambda i, j: (i, j, 0))

    blk = (2 * _nbytes((ln, MQK), BF16) + 2 * _nbytes((ln, MV), BF16) + _nbytes((ln, MV), F32)
           + _nbytes((ln, GATE_LANES), F32) + _nbytes((1, MV), F32))
    scr = _nbytes((MLSTM_HEADS, MLSTM_DQK, MLSTM_DV), F32) + 2 * _nbytes((SUBLANES, LANES), F32)
    return pl.pallas_call(
        _mlstm_kernel,
        grid=(b, t // ln),
        in_specs=[rows(MQK), rows(MQK), rows(MV), rows(MV), rows(GATE_LANES),
                  pl.BlockSpec((1, MV), lambda i, j: (0, 0))],
        out_specs=rows(MV),
        out_shape=jax.ShapeDtypeStruct((b, t, MV), BF16),
        scratch_shapes=[pltpu.VMEM((MLSTM_HEADS, MLSTM_DQK, MLSTM_DV), F32),
                        pltpu.VMEM((SUBLANES, LANES), F32),
                        pltpu.VMEM((SUBLANES, LANES), F32)],
        compiler_params=pltpu.CompilerParams(
            dimension_semantics=("arbitrary", "arbitrary"),
            vmem_limit_bytes=_vmem_limit(blk, scr, 8 << 20)),
        name="mlstm_chunkwise",
    )(qm, km, vm, om, gt, hnorm_g.reshape(1, MV))


def _sb_kernel(q_ref, k_ref, v_ref, out_ref):
    blk = SB_BLOCK
    i = pl.program_id(2)
    q = q_ref[0]
    scale = SB_DH ** -0.5
    ri = lax.broadcasted_iota(jnp.int32, (blk, blk), 0)
    ci = lax.broadcasted_iota(jnp.int32, (blk, blk), 1)
    suffix = (ri >= ci).astype(BF16)
    t_idx = i * blk + ri

    def cond(carry):
        j, across, _ = carry
        return jnp.logical_and(j >= 0, jnp.max(across) > F32_EXP_UNDERFLOW)

    def body(carry):
        j, across, acc = carry
        start = pl.multiple_of(j * blk, blk)
        kj = k_ref[0, pl.ds(start, blk), :]
        vj = v_ref[0, pl.ds(start, blk), :]
        z = _dot_nt(q, kj) * scale
        s_idx = j * blk + ci
        mask = jnp.logical_and(s_idx < t_idx, s_idx >= PAD_FRONT)
        log1m = jnp.where(mask, -_softplus(z), 0.0)
        hi, mid, lo = _split3(log1m)
        within = _dot(hi, suffix) + _dot(mid, suffix) + _dot(lo, suffix)
        w = jnp.exp(jnp.where(mask, z + within + across, NEG))
        acc = acc + _dot(w.astype(BF16), vj)
        return j - 1, across + within[:, 0:1], acc

    init = (i, jnp.zeros((blk, 1), F32), jnp.zeros((blk, SB_DH), F32))
    _, _, acc = lax.while_loop(cond, body, init)
    out_ref[0] = acc.astype(BF16)


def _stick_breaking(qs, ks, vs):
    b, t, _ = qs.shape
    blk = SB_BLOCK
    blk_bytes = 2 * _nbytes((blk, SB_DH), BF16) + 2 * _nbytes((t, SB_DH), BF16)
    return pl.pallas_call(
        _sb_kernel,
        grid=(b, SB_HEADS, t // blk),
        in_specs=[pl.BlockSpec((1, blk, SB_DH), lambda bi, hi, qi: (bi, qi, hi)),
                  pl.BlockSpec((1, t, SB_DH), lambda bi, hi, qi: (bi, 0, hi)),
                  pl.BlockSpec((1, t, SB_DH), lambda bi, hi, qi: (bi, 0, hi))],
        out_specs=pl.BlockSpec((1, blk, SB_DH), lambda bi, hi, qi: (bi, qi, hi)),
        out_shape=jax.ShapeDtypeStruct((b, t, SBW), BF16),
        compiler_params=pltpu.CompilerParams(
            dimension_semantics=("arbitrary", "arbitrary", "arbitrary"),
            vmem_limit_bytes=_vmem_limit(blk_bytes, 0, 4 << 20)),
        name="stick_breaking",
    )(qs, ks, vs)


def _out_proj_kernel(h_ref, hm_ref, hs_ref, wm_ref, ws_ref, g_ref, out_ref):
    y = _dot(hm_ref[...], wm_ref[...]) + _dot(hs_ref[...], ws_ref[...])
    out_ref[...] = h_ref[...] + _rms(y, g_ref[...])


def _out_proj(h2, hm2, hs2, w_out, g1):
    n, d = h2.shape
    tm = _seq_tile(n)
    wb = w_out.astype(BF16)
    wm, ws = wb[:MV], wb[MV:]
    blk = (2 * _nbytes((tm, d), F32) + _nbytes((tm, MV), BF16) + _nbytes((tm, SBW), BF16)
           + _nbytes(wb.shape, BF16) + _nbytes((1, d), F32))
    return pl.pallas_call(
        _out_proj_kernel,
        grid=(n // tm,),
        in_specs=[pl.BlockSpec((tm, d), lambda i: (i, 0)),
                  pl.BlockSpec((tm, MV), lambda i: (i, 0)),
                  pl.BlockSpec((tm, SBW), lambda i: (i, 0)),
                  pl.BlockSpec(wm.shape, lambda i: (0, 0)),
                  pl.BlockSpec(ws.shape, lambda i: (0, 0)),
                  pl.BlockSpec((1, d), lambda i: (0, 0))],
        out_specs=pl.BlockSpec((tm, d), lambda i: (i, 0)),
        out_shape=jax.ShapeDtypeStruct((n, d), F32),
        compiler_params=pltpu.CompilerParams(
            dimension_semantics=("arbitrary",),
            vmem_limit_bytes=_vmem_limit(blk, 0, 2 * _nbytes((tm, d), F32))),
        name="mixer_out_proj",
    )(h2, hm2, hs2, wm, ws, g1.reshape(1, d))


def _conformer_kernel(h_ref, g0_ref, w1_ref, b1_ref, wdw_ref, bdw_ref, lng_ref, lnb_ref,
                      w2_ref, b2_ref, g1_ref, out_ref, conv_scr):
    t = pl.program_id(1)
    tm, d = h_ref.shape[1], h_ref.shape[2]
    x = h_ref[0]
    ub = _rms(x, g0_ref[...]).astype(BF16)
    ag = _dot(ub, w1_ref[...]) + b1_ref[...]
    row = t * tm + lax.broadcasted_iota(jnp.int32, (tm, 1), 0)
    y = jnp.where(row >= PAD_FRONT, ag[:, :d] * jax.nn.sigmoid(ag[:, d:]), 0.0)

    @pl.when(t == 0)
    def _():
        conv_scr[0:CONV_HALO, :] = jnp.zeros((CONV_HALO, d), F32)

    conv_scr[CONV_HALO:CONV_HALO + tm, :] = y
    acc = bdw_ref[...]
    for j in range(CONV_WIDTH):
        shift = CONV_HALO - (CONV_WIDTH - 1) + j
        acc = acc + wdw_ref[j:j + 1, :] * conv_scr[pl.ds(shift, tm), :]
    conv_scr[0:CONV_HALO, :] = conv_scr[tm:tm + CONV_HALO, :]

    mu = jnp.mean(acc, axis=-1, keepdims=True)
    cen = acc - mu
    var = jnp.mean(cen * cen, axis=-1, keepdims=True)
    ln = cen * lax.rsqrt(var + EPS) * lng_ref[...] + lnb_ref[...]
    act = (ln * jax.nn.sigmoid(ln)).astype(BF16)
    z = _dot(act, w2_ref[...]) + b2_ref[...]
    out_ref[0] = x + _rms(z, g1_ref[...])


def _conformer(h, g0, g1, w_pw1, b_pw1, w_dw, b_dw, ln_g, ln_b, w_pw2, b_pw2):
    b, t, d = h.shape
    tm = _seq_tile(t)
    ins = [h, g0.reshape(1, d), w_pw1.astype(BF16), b_pw1.reshape(1, -1), w_dw, b_dw.reshape(1, d),
           ln_g.reshape(1, d), ln_b.reshape(1, d), w_pw2.astype(BF16), b_pw2.reshape(1, d), g1.reshape(1, d)]

    def full(a):
        return pl.BlockSpec(a.shape, lambda i, j: (0,) * a.ndim)

    blk = 2 * _nbytes((tm, d), F32) + sum(_nbytes(a.shape, a.dtype) for a in ins[1:])
    scr = _nbytes((tm + CONV_HALO, d), F32)
    return pl.pallas_call(
        _conformer_kernel,
        grid=(b, t // tm),
        in_specs=[pl.BlockSpec((1, tm, d), lambda i, j: (i, j, 0))] + [full(a) for a in ins[1:]],
        out_specs=pl.BlockSpec((1, tm, d), lambda i, j: (i, j, 0)),
        out_shape=jax.ShapeDtypeStruct((b, t, d), F32),
        scratch_shapes=[pltpu.VMEM((tm + CONV_HALO, d), F32)],
        compiler_params=pltpu.CompilerParams(
            dimension_semantics=("arbitrary", "arbitrary"),
            vmem_limit_bytes=_vmem_limit(blk, scr, 6 * _nbytes((tm, d), F32))),
        name="conformer_conv",
    )(*ins)


def _ffn_kernel(h_ref, g2_ref, wg_ref, wu_ref, wd_ref, g3_ref, out_ref, u_scr, acc_scr):
    c = pl.program_id(1)

    @pl.when(c == 0)
    def _():
        u_scr[...] = _rms(h_ref[...], g2_ref[...]).astype(BF16)
        acc_scr[...] = jnp.zeros(acc_scr.shape, F32)

    u = u_scr[...]
    a = _dot(u, wg_ref[...])
    hid = (a * jax.nn.sigmoid(a) * _dot(u, wu_ref[...])).astype(BF16)
    acc_scr[...] += _dot(hid, wd_ref[...])

    @pl.when(c == pl.num_programs(1) - 1)
    def _():
        out_ref[...] = h_ref[...] + _rms(acc_scr[...], g3_ref[...])


def _ffn_chunk(hidden):
    best = LANES
    for cand in range(LANES, 1408 + 1, LANES):
        if hidden % cand == 0:
            best = cand
    return best


def _ffn(h2, g2, g3, w_gate, w_up, w_down):
    n, d = h2.shape
    hidden = w_gate.shape[1]
    tm = _seq_tile(n)
    ck = _ffn_chunk(hidden)
    blk = (2 * _nbytes((tm, d), F32) + 3 * _nbytes((d, ck), BF16) + 2 * _nbytes((1, d), F32))
    scr = _nbytes((tm, d), BF16) + _nbytes((tm, d), F32)
    return pl.pallas_call(
        _ffn_kernel,
        grid=(n // tm, hidden // ck),
        in_specs=[pl.BlockSpec((tm, d), lambda i, c: (i, 0)),
                  pl.BlockSpec((1, d), lambda i, c: (0, 0)),
                  pl.BlockSpec((d, ck), lambda i, c: (0, c)),
                  pl.BlockSpec((d, ck), lambda i, c: (0, c)),
                  pl.BlockSpec((ck, d), lambda i, c: (c, 0)),
                  pl.BlockSpec((1, d), lambda i, c: (0, 0))],
        out_specs=pl.BlockSpec((tm, d), lambda i, c: (i, 0)),
        out_shape=jax.ShapeDtypeStruct((n, d), F32),
        scratch_shapes=[pltpu.VMEM((tm, d), BF16), pltpu.VMEM((tm, d), F32)],
        compiler_params=pltpu.CompilerParams(
            dimension_semantics=("arbitrary", "arbitrary"),
            vmem_limit_bytes=_vmem_limit(blk, scr, 3 * _nbytes((tm, ck), F32))),
        name="swiglu_ffn",
    )(h2, g2.reshape(1, d), w_gate.astype(BF16), w_up.astype(BF16), w_down.astype(BF16), g3.reshape(1, d))


def kernel(x, meta, norm_g, mix_w_in, mix_qk_conv_w, mix_qk_conv_b, mix_gate_b, mix_hnorm_g, mix_w_out,
           conv_w_pw1, conv_b_pw1, conv_w_dw, conv_b_dw, conv_ln_g, conv_ln_b, conv_w_pw2, conv_b_pw2,
           ffn_w_gate, ffn_w_up, ffn_w_down):
    b, seq, d = x.shape
    depth = norm_g.shape[0]
    t = seq + N_META + PAD_FRONT
    h = jnp.concatenate([jnp.zeros((b, PAD_FRONT, d), x.dtype),
                         jnp.broadcast_to(meta[None].astype(x.dtype), (b, N_META, d)), x], axis=1)
    for layer in range(depth):
        g = norm_g[layer]
        i = layer // 2
        if layer % 2 == 0:
            qm, km, vm, om, gt, qs, ks, vs = _in_proj(h, g[0], mix_w_in[i], mix_qk_conv_w[i],
                                                      mix_qk_conv_b[i], mix_gate_b[i])
            hm = _mlstm(qm, km, vm, om, gt, mix_hnorm_g[i])
            hs = _stick_breaking(qs, ks, vs)
            h2 = _out_proj(h.reshape(b * t, d), hm.reshape(b * t, MV), hs.reshape(b * t, SBW),
                           mix_w_out[i], g[1])
        else:
            h2 = _conformer(h, g[0], g[1], conv_w_pw1[i], conv_b_pw1[i], conv_w_dw[i], conv_b_dw[i],
                            conv_ln_g[i], conv_ln_b[i], conv_w_pw2[i], conv_b_pw2[i]).reshape(b * t, d)
        h = _ffn(h2, g[2], g[3], ffn_w_gate[layer], ffn_w_up[layer], ffn_w_down[layer]).reshape(b, t, d)
    return h[:, N_META + PAD_FRONT:]
```

```python
import functools

import jax
import jax.numpy as jnp
from jax import lax
from jax.experimental import pallas as pl
from jax.experimental.pallas import tpu as pltpu

N_META = 16
MLSTM_HEADS = 4
MLSTM_DQK = 128
MLSTM_DV = 256
QK_CONV_WIDTH = 4
GATE_SOFTCAP = 15.0
SB_HEADS = 4
SB_DH = 128
SB_BLOCK = 128
PAD_FRONT = SB_BLOCK - N_META
CONV_WIDTH = 31
MQK = MLSTM_HEADS * MLSTM_DQK
MV = MLSTM_HEADS * MLSTM_DV
SBW = SB_HEADS * SB_DH
NEG = -1e30
EPS = 1e-6

LANES = 128
SUBLANES = 8
V7X_VMEM_BYTES = 64 * 1024 * 1024
F32_EXP_UNDERFLOW = -104.0

MLSTM_CHUNK = 128
CONV_HALO = 32
GATE_LANES = LANES

BF16 = jnp.bfloat16
F32 = jnp.float32


def _vmem_limit(block_bytes, scratch_bytes=0, temp_bytes=0):
    est = 2 * block_bytes + scratch_bytes + temp_bytes + (4 << 20)
    return int(min(max(est, 16 << 20), V7X_VMEM_BYTES - (6 << 20)))


def _nbytes(shape, dtype):
    n = 1
    for s in shape:
        n *= s
    return n * jnp.dtype(dtype).itemsize


def _seq_tile(t):
    best = SB_BLOCK
    for cand in range(SB_BLOCK, 1024 + 1, SB_BLOCK):
        if t % cand == 0:
            best = cand
    return best


def _rms(x, g):
    return x * lax.rsqrt(jnp.mean(x * x, axis=-1, keepdims=True) + EPS) * g


def _softplus(x):
    return jnp.maximum(x, 0.0) + jnp.log(1.0 + jnp.exp(-jnp.abs(x)))


def _split3(x):
    hi = x.astype(BF16)
    r = x - hi.astype(F32)
    mid = r.astype(BF16)
    lo = (r - mid.astype(F32)).astype(BF16)
    return hi, mid, lo


def _dot(a, b):
    return jnp.dot(a, b, preferred_element_type=F32)


def _dot_nt(a, b):
    return lax.dot_general(a, b, (((1,), (1,)), ((), ())), preferred_element_type=F32)


def _dot_tn(a, b):
    return lax.dot_general(a, b, (((0,), (0,)), ((), ())), preferred_element_type=F32)


def _in_proj_kernel(h_ref, g_ref, wqk_ref, wv_ref, wo_ref, wg_ref, wsq_ref, wsk_ref, wsv_ref,
                    cw_ref, cb_ref, gb_ref,
                    qm_ref, km_ref, vm_ref, om_ref, gt_ref, qs_ref, ks_ref, vs_ref, conv_scr):
    t = pl.program_id(1)
    tm = h_ref.shape[1]
    u = _rms(h_ref[0], g_ref[...])
    row = t * tm + lax.broadcasted_iota(jnp.int32, (tm, 1), 0)
    valid = row >= PAD_FRONT
    ub = jnp.where(valid, u, 0.0).astype(BF16)

    @pl.when(t == 0)
    def _():
        conv_scr[0:SUBLANES, :] = jnp.zeros((SUBLANES, 2 * MQK), F32)

    conv_scr[SUBLANES:SUBLANES + tm, :] = _dot(ub, wqk_ref[...])
    acc = cb_ref[...]
    for j in range(QK_CONV_WIDTH):
        shift = SUBLANES - (QK_CONV_WIDTH - 1) + j
        acc = acc + cw_ref[j:j + 1, :] * conv_scr[pl.ds(shift, tm), :]
    conv_scr[0:SUBLANES, :] = conv_scr[tm:tm + SUBLANES, :]
    qk = acc * jax.nn.sigmoid(acc)
    qm_ref[0] = (qk[:, :MQK] * MLSTM_DQK ** -0.5).astype(BF16)
    km_ref[0] = qk[:, MQK:].astype(BF16)

    vm_ref[0] = _dot(ub, wv_ref[...]).astype(BF16)
    om_ref[0] = _dot(ub, wo_ref[...])
    qs_ref[0] = (_dot(ub, wsq_ref[...]) * SB_DH ** -0.5).astype(BF16)
    ks_ref[0] = _dot(ub, wsk_ref[...]).astype(BF16)
    vs_ref[0] = _dot(ub, wsv_ref[...]).astype(BF16)

    gt = _dot(ub, wg_ref[...]) + gb_ref[...]
    gt = GATE_SOFTCAP * jnp.tanh(gt / GATE_SOFTCAP)
    lane = lax.broadcasted_iota(jnp.int32, (1, GATE_LANES), 1)
    log_i = jnp.where(valid, gt, NEG)
    log_f = jnp.where(valid, -_softplus(-gt), 0.0)
    gt_ref[0] = jnp.where(lane < MLSTM_HEADS, log_i, jnp.where(lane < 2 * MLSTM_HEADS, log_f, 0.0))


def _in_proj(h, g0, w_in, conv_w, conv_b, gate_b):
    b, t, d = h.shape
    tm = _seq_tile(t)
    o = [0, 2 * MQK, 2 * MQK + MV, 2 * MQK + 2 * MV, 2 * MQK + 2 * MV + 2 * MLSTM_HEADS]
    o += [o[-1] + SBW, o[-1] + 2 * SBW, o[-1] + 3 * SBW]
    wb = w_in.astype(BF16)
    wqk, wv, wo = wb[:, o[0]:o[1]], wb[:, o[1]:o[2]], wb[:, o[2]:o[3]]
    wg = jnp.pad(wb[:, o[3]:o[4]], ((0, 0), (0, GATE_LANES - 2 * MLSTM_HEADS)))
    wsq, wsk, wsv = wb[:, o[4]:o[5]], wb[:, o[5]:o[6]], wb[:, o[6]:o[7]]
    gb = jnp.pad(gate_b.astype(F32), (0, GATE_LANES - 2 * MLSTM_HEADS)).reshape(1, GATE_LANES)

    def full(a):
        return pl.BlockSpec(a.shape, lambda i, j: (0,) * a.ndim)

    def rows(width):
        return pl.BlockSpec((1, tm, width), lambda i, j: (i, j, 0))

    ins = [h, g0.reshape(1, d), wqk, wv, wo, wg, wsq, wsk, wsv, conv_w, conv_b.reshape(1, -1), gb]
    out_widths = [(MQK, BF16), (MQK, BF16), (MV, BF16), (MV, F32), (GATE_LANES, F32),
                  (SBW, BF16), (SBW, BF16), (SBW, BF16)]
    blk = _nbytes((tm, d), F32) + sum(_nbytes(a.shape, a.dtype) for a in ins[1:])
    blk += sum(_nbytes((tm, w), dt) for w, dt in out_widths)
    scr = _nbytes((tm + 2 * SUBLANES, 2 * MQK), F32)
    return pl.pallas_call(
        _in_proj_kernel,
        grid=(b, t // tm),
        in_specs=[rows(d)] + [full(a) for a in ins[1:]],
        out_specs=[rows(w) for w, _ in out_widths],
        out_shape=[jax.ShapeDtypeStruct((b, t, w), dt) for w, dt in out_widths],
        scratch_shapes=[pltpu.VMEM((tm + 2 * SUBLANES, 2 * MQK), F32)],
        compiler_params=pltpu.CompilerParams(
            dimension_semantics=("arbitrary", "arbitrary"),
            vmem_limit_bytes=_vmem_limit(blk, scr, 4 * _nbytes((tm, 2 * MQK), F32))),
        name="mixer_in_proj",
    )(*ins)


def _mlstm_kernel(q_ref, k_ref, v_ref, o_ref, gt_ref, hg_ref, out_ref, c_scr, n_scr):
    n_heads, dk, dv, ln = MLSTM_HEADS, MLSTM_DQK, MLSTM_DV, MLSTM_CHUNK
    heads = range(n_heads)

    @pl.when(pl.program_id(1) == 0)
    def _():
        c_scr[...] = jnp.zeros(c_scr.shape, F32)
        n_scr[...] = jnp.zeros(n_scr.shape, F32)

    gates = gt_ref[0]
    ri = lax.broadcasted_iota(jnp.int32, (ln, ln), 0)
    ci = lax.broadcasted_iota(jnp.int32, (ln, ln), 1)
    tril = ci <= ri
    ltri = tril.astype(BF16)
    g_hi, g_mid, g_lo = _split3(gates)
    csum = _dot(ltri, g_hi) + _dot(ltri, g_mid) + _dot(ltri, g_lo)
    gates_t = gates.T
    csum_t = csum.T

    q = [q_ref[0, :, hd * dk:(hd + 1) * dk] for hd in heads]
    k = [k_ref[0, :, hd * dk:(hd + 1) * dk] for hd in heads]
    v = [v_ref[0, :, hd * dv:(hd + 1) * dv] for hd in heads]
    c_st = [c_scr[hd] for hd in heads]
    n_st = [n_scr[hd:hd + 1, :] for hd in heads]
    bcol = [csum[:, n_heads + hd:n_heads + hd + 1] for hd in heads]
    s, q_c = [], []
    for hd in heads:
        brow = csum_t[n_heads + hd:n_heads + hd + 1, :]
        li_row = gates_t[hd:hd + 1, :]
        w_intra = jnp.exp(jnp.where(tril, bcol[hd] - brow + li_row, NEG))
        s.append(_dot_nt(q[hd], k[hd]) * w_intra)
        q_c.append(_dot(q[hd], c_st[hd].astype(BF16)))
    s_v = [_dot(s[hd].astype(BF16), v[hd]) for hd in heads]

    for hd in heads:
        w_inter = jnp.exp(bcol[hd])
        num = s_v[hd] + w_inter * q_c[hd]
        den = (jnp.sum(s[hd], axis=-1, keepdims=True)
               + w_inter * jnp.sum(q[hd].astype(F32) * n_st[hd], axis=-1, keepdims=True))
        h_out = num / jnp.maximum(jnp.abs(den), 1.0)
        hn = _rms(h_out, hg_ref[:, hd * dv:(hd + 1) * dv])
        gate = jax.nn.sigmoid(o_ref[0, :, hd * dv:(hd + 1) * dv])
        out_ref[0, :, hd * dv:(hd + 1) * dv] = (hn * gate).astype(BF16)

    for hd in heads:
        g_tot = csum[ln - 1:ln, n_heads + hd:n_heads + hd + 1]
        wa = jnp.exp(g_tot - bcol[hd] + gates[:, hd:hd + 1])
        wc = jnp.exp(g_tot)
        kw = k[hd].astype(F32) * wa
        c_scr[hd] = wc * c_st[hd] + _dot_tn(kw.astype(BF16), v[hd])
        n_scr[hd:hd + 1, :] = wc * n_st[hd] + jnp.sum(kw, axis=0, keepdims=True)


def _mlstm(qm, km, vm, om, gt, hnorm_g):
    b, t, _ = qm.shape
    ln = MLSTM_CHUNK

    def rows(width):
        return pl.BlockSpec((1, ln, width), lambda i, j: (i, j, 0))

    blk = (2 * _nbytes((ln, MQK), BF16) + 2 * _nbytes((ln, MV), BF16) + _nbytes((ln, MV), F32)
           + _nbytes((ln, GATE_LANES), F32) + _nbytes((1, MV), F32))
    scr = _nbytes((MLSTM_HEADS, MLSTM_DQK, MLSTM_DV), F32) + _nbytes((SUBLANES, LANES), F32)
    return pl.pallas_call(
        _mlstm_kernel,
        grid=(b, t // ln),
        in_specs=[rows(MQK), rows(MQK), rows(MV), rows(MV), rows(GATE_LANES),
                  pl.BlockSpec((1, MV), lambda i, j: (0, 0))],
        out_specs=rows(MV),
        out_shape=jax.ShapeDtypeStruct((b, t, MV), BF16),
        scratch_shapes=[pltpu.VMEM((MLSTM_HEADS, MLSTM_DQK, MLSTM_DV), F32),
                        pltpu.VMEM((SUBLANES, LANES), F32)],
        compiler_params=pltpu.CompilerParams(
            dimension_semantics=("arbitrary", "arbitrary"),
            vmem_limit_bytes=_vmem_limit(blk, scr, 8 << 20)),
        name="mlstm_chunkwise",
    )(qm, km, vm, om, gt, hnorm_g.reshape(1, MV))


def _sb_block(q_ref, k_ref, v_ref, cmat_ref, across_scr, acc_scr, i, j, *, causal, validity, first):
    blk = SB_BLOCK
    start = pl.multiple_of(j * blk, blk)
    mask = None
    if causal or validity:
        ri = lax.broadcasted_iota(jnp.int32, (blk, blk), 0)
        s_idx = j * blk + lax.broadcasted_iota(jnp.int32, (blk, blk), 1)
        if causal:
            mask = s_idx < i * blk + ri
        if validity:
            ok = s_idx >= PAD_FRONT
            mask = ok if mask is None else jnp.logical_and(mask, ok)
    heads = range(SB_HEADS)
    cols = [slice(hd * SB_DH, (hd + 1) * SB_DH) for hd in heads]
    z = [_dot_nt(q_ref[0, :, c], k_ref[0, pl.ds(start, blk), c]) for c in cols]
    both = []
    for hd in heads:
        log1m = -_softplus(z[hd])
        if mask is not None:
            log1m = jnp.where(mask, log1m, 0.0)
        hi = log1m.astype(BF16)
        lo = (log1m - hi.astype(F32)).astype(BF16)
        both.append(_dot(jnp.concatenate([hi, lo], axis=1), cmat_ref[...]))
    prev = [None if first else across_scr[hd] for hd in heads]
    pv = []
    for hd in heads:
        log_w = z[hd] + both[hd][:, :blk]
        if not first:
            log_w = log_w + prev[hd]
        if mask is not None:
            log_w = jnp.where(mask, log_w, NEG)
        pv.append(_dot(jnp.exp(log_w).astype(BF16), v_ref[0, pl.ds(start, blk), cols[hd]]))
    amax = None
    for hd in heads:
        across = both[hd][:, blk:]
        if first:
            acc_scr[:, cols[hd]] = pv[hd]
        else:
            acc_scr[:, cols[hd]] += pv[hd]
            across = across + prev[hd]
        across_scr[hd] = across
        amax = across if amax is None else jnp.maximum(amax, across)
    return (jnp.max(amax) > F32_EXP_UNDERFLOW).astype(jnp.int32)


def _sb_kernel(q_ref, k_ref, v_ref, out_ref, cmat_ref, across_scr, acc_scr):
    blk = SB_BLOCK
    i = pl.program_id(1)

    @pl.when(jnp.logical_and(pl.program_id(0) == 0, i == 0))
    def _():
        ri = lax.broadcasted_iota(jnp.int32, (2 * blk, 2 * blk), 0)
        ci = lax.broadcasted_iota(jnp.int32, (2 * blk, 2 * blk), 1)
        cmat_ref[...] = jnp.logical_or(ci >= blk, (ri % blk) >= ci).astype(BF16)

    block = functools.partial(_sb_block, q_ref, k_ref, v_ref, cmat_ref, across_scr, acc_scr, i)
    go = block(i, causal=True, validity=True, first=True)

    def cond(carry):
        j, go = carry
        return jnp.logical_and(j >= 1, go > 0)

    def body(carry):
        j, _ = carry
        return j - 1, block(j, causal=False, validity=False, first=False)

    j, go = lax.while_loop(cond, body, (i - 1, go))

    @pl.when(jnp.logical_and(j == 0, go > 0))
    def _():
        block(0, causal=False, validity=True, first=False)

    out_ref[0] = acc_scr[...].astype(BF16)


def _stick_breaking(qs, ks, vs):
    b, t, _ = qs.shape
    blk = SB_BLOCK
    scr = (_nbytes((2 * blk, 2 * blk), BF16) + _nbytes((SB_HEADS, blk, blk), F32) + _nbytes((blk, SBW), F32))
    vmem = 4 * _nbytes((blk, SBW), BF16) + 2 * _nbytes((t, SBW), BF16) + scr + (8 << 20)
    resident = dict(pipeline_mode=pl.Buffered(1))
    return pl.pallas_call(
        _sb_kernel,
        grid=(b, t // blk),
        in_specs=[pl.BlockSpec((1, blk, SBW), lambda bi, qi: (bi, qi, 0)),
                  pl.BlockSpec((1, t, SBW), lambda bi, qi: (bi, 0, 0), **resident),
                  pl.BlockSpec((1, t, SBW), lambda bi, qi: (bi, 0, 0), **resident)],
        out_specs=pl.BlockSpec((1, blk, SBW), lambda bi, qi: (bi, qi, 0)),
        out_shape=jax.ShapeDtypeStruct((b, t, SBW), BF16),
        scratch_shapes=[pltpu.VMEM((2 * blk, 2 * blk), BF16),
                        pltpu.VMEM((SB_HEADS, blk, blk), F32),
                        pltpu.VMEM((blk, SBW), F32)],
        compiler_params=pltpu.CompilerParams(
            dimension_semantics=("arbitrary", "arbitrary"),
            vmem_limit_bytes=int(vmem)),
        name="stick_breaking",
    )(qs, ks, vs)


def _out_proj_kernel(h_ref, hm_ref, hs_ref, wm_ref, ws_ref, g_ref, out_ref):
    y = _dot(hm_ref[...], wm_ref[...]) + _dot(hs_ref[...], ws_ref[...])
    out_ref[...] = h_ref[...] + _rms(y, g_ref[...])


def _out_proj(h2, hm2, hs2, w_out, g1):
    n, d = h2.shape
    tm = _seq_tile(n)
    wb = w_out.astype(BF16)
    wm, ws = wb[:MV], wb[MV:]
    blk = (2 * _nbytes((tm, d), F32) + _nbytes((tm, MV), BF16) + _nbytes((tm, SBW), BF16)
           + _nbytes(wb.shape, BF16) + _nbytes((1, d), F32))
    return pl.pallas_call(
        _out_proj_kernel,
        grid=(n // tm,),
        in_specs=[pl.BlockSpec((tm, d), lambda i: (i, 0)),
                  pl.BlockSpec((tm, MV), lambda i: (i, 0)),
                  pl.BlockSpec((tm, SBW), lambda i: (i, 0)),
                  pl.BlockSpec(wm.shape, lambda i: (0, 0)),
                  pl.BlockSpec(ws.shape, lambda i: (0, 0)),
                  pl.BlockSpec((1, d), lambda i: (0, 0))],
        out_specs=pl.BlockSpec((tm, d), lambda i: (i, 0)),
        out_shape=jax.ShapeDtypeStruct((n, d), F32),
        compiler_params=pltpu.CompilerParams(
            dimension_semantics=("arbitrary",),
            vmem_limit_bytes=_vmem_limit(blk, 0, 2 * _nbytes((tm, d), F32))),
        name="mixer_out_proj",
    )(h2, hm2, hs2, wm, ws, g1.reshape(1, d))


def _conformer_kernel(h_ref, g0_ref, w1_ref, b1_ref, wdw_ref, bdw_ref, lng_ref, lnb_ref,
                      w2_ref, b2_ref, g1_ref, out_ref, conv_scr):
    t = pl.program_id(1)
    tm, d = h_ref.shape[1], h_ref.shape[2]
    x = h_ref[0]
    ub = _rms(x, g0_ref[...]).astype(BF16)
    ag = _dot(ub, w1_ref[...]) + b1_ref[...]
    row = t * tm + lax.broadcasted_iota(jnp.int32, (tm, 1), 0)
    y = jnp.where(row >= PAD_FRONT, ag[:, :d] * jax.nn.sigmoid(ag[:, d:]), 0.0)

    @pl.when(t == 0)
    def _():
        conv_scr[0:CONV_HALO, :] = jnp.zeros((CONV_HALO, d), F32)

    conv_scr[CONV_HALO:CONV_HALO + tm, :] = y
    acc = bdw_ref[...]
    for j in range(CONV_WIDTH):
        shift = CONV_HALO - (CONV_WIDTH - 1) + j
        acc = acc + wdw_ref[j:j + 1, :] * conv_scr[pl.ds(shift, tm), :]
    conv_scr[0:CONV_HALO, :] = conv_scr[tm:tm + CONV_HALO, :]

    mu = jnp.mean(acc, axis=-1, keepdims=True)
    cen = acc - mu
    var = jnp.mean(cen * cen, axis=-1, keepdims=True)
    ln = cen * lax.rsqrt(var + EPS) * lng_ref[...] + lnb_ref[...]
    act = (ln * jax.nn.sigmoid(ln)).astype(BF16)
    z = _dot(act, w2_ref[...]) + b2_ref[...]
    out_ref[0] = x + _rms(z, g1_ref[...])


def _conformer(h, g0, g1, w_pw1, b_pw1, w_dw, b_dw, ln_g, ln_b, w_pw2, b_pw2):
    b, t, d = h.shape
    tm = _seq_tile(t)
    ins = [h, g0.reshape(1, d), w_pw1.astype(BF16), b_pw1.reshape(1, -1), w_dw, b_dw.reshape(1, d),
           ln_g.reshape(1, d), ln_b.reshape(1, d), w_pw2.astype(BF16), b_pw2.reshape(1, d), g1.reshape(1, d)]

    def full(a):
        return pl.BlockSpec(a.shape, lambda i, j: (0,) * a.ndim)

    blk = 2 * _nbytes((tm, d), F32) + sum(_nbytes(a.shape, a.dtype) for a in ins[1:])
    scr = _nbytes((tm + CONV_HALO, d), F32)
    return pl.pallas_call(
        _conformer_kernel,
        grid=(b, t // tm),
        in_specs=[pl.BlockSpec((1, tm, d), lambda i, j: (i, j, 0))] + [full(a) for a in ins[1:]],
        out_specs=pl.BlockSpec((1, tm, d), lambda i, j: (i, j, 0)),
        out_shape=jax.ShapeDtypeStruct((b, t, d), F32),
        scratch_shapes=[pltpu.VMEM((tm + CONV_HALO, d), F32)],
        compiler_params=pltpu.CompilerParams(
            dimension_semantics=("arbitrary", "arbitrary"),
            vmem_limit_bytes=_vmem_limit(blk, scr, 6 * _nbytes((tm, d), F32))),
        name="conformer_conv",
    )(*ins)


def _ffn_kernel(h_ref, g2_ref, wg_ref, wu_ref, wd_ref, g3_ref, out_ref, u_scr, acc_scr):
    c = pl.program_id(1)

    @pl.when(c == 0)
    def _():
        u_scr[...] = _rms(h_ref[...], g2_ref[...]).astype(BF16)
        acc_scr[...] = jnp.zeros(acc_scr.shape, F32)

    u = u_scr[...]
    a = _dot(u, wg_ref[...])
    hid = (a * jax.nn.sigmoid(a) * _dot(u, wu_ref[...])).astype(BF16)
    acc_scr[...] += _dot(hid, wd_ref[...])

    @pl.when(c == pl.num_programs(1) - 1)
    def _():
        out_ref[...] = h_ref[...] + _rms(acc_scr[...], g3_ref[...])


def _ffn_chunk(hidden):
    best = LANES
    for cand in range(LANES, 1408 + 1, LANES):
        if hidden % cand == 0:
            best = cand
    return best


def _ffn(h2, g2, g3, w_gate, w_up, w_down):
    n, d = h2.shape
    hidden = w_gate.shape[1]
    tm = _seq_tile(n)
    ck = _ffn_chunk(hidden)
    blk = (2 * _nbytes((tm, d), F32) + 3 * _nbytes((d, ck), BF16) + 2 * _nbytes((1, d), F32))
    scr = _nbytes((tm, d), BF16) + _nbytes((tm, d), F32)
    return pl.pallas_call(
        _ffn_kernel,
        grid=(n // tm, hidden // ck),
        in_specs=[pl.BlockSpec((tm, d), lambda i, c: (i, 0)),
                  pl.BlockSpec((1, d), lambda i, c: (0, 0)),
                  pl.BlockSpec((d, ck), lambda i, c: (0, c)),
                  pl.BlockSpec((d, ck), lambda i, c: (0, c)),
                  pl.BlockSpec((ck, d), lambda i, c: (c, 0)),
                  pl.BlockSpec((1, d), lambda i, c: (0, 0))],
        out_specs=pl.BlockSpec((tm, d), lambda i, c: (i, 0)),
        out_shape=jax.ShapeDtypeStruct((n, d), F32),
        scratch_shapes=[pltpu.VMEM((tm, d), BF16), pltpu.VMEM((tm, d), F32)],
        compiler_params=pltpu.CompilerParams(
            dimension_semantics=("arbitrary", "arbitrary"),
            vmem_limit_bytes=_vmem_limit(blk, scr, 3 * _nbytes((tm, ck), F32))),
        name="swiglu_ffn",
    )(h2, g2.reshape(1, d), w_gate.astype(BF16), w_up.astype(BF16), w_down.astype(BF16), g3.reshape(1, d))


def kernel(x, meta, norm_g, mix_w_in, mix_qk_conv_w, mix_qk_conv_b, mix_gate_b, mix_hnorm_g, mix_w_out,
           conv_w_pw1, conv_b_pw1, conv_w_dw, conv_b_dw, conv_ln_g, conv_ln_b, conv_w_pw2, conv_b_pw2,
           ffn_w_gate, ffn_w_up, ffn_w_down):
    b, seq, d = x.shape
    depth = norm_g.shape[0]
    t = seq + N_META + PAD_FRONT
    h = jnp.concatenate([jnp.zeros((b, PAD_FRONT, d), x.dtype),
                         jnp.broadcast_to(meta[None].astype(x.dtype), (b, N_META, d)), x], axis=1)
    for layer in range(depth):
        g = norm_g[layer]
        i = layer // 2
        if layer % 2 == 0:
            qm, km, vm, om, gt, qs, ks, vs = _in_proj(h, g[0], mix_w_in[i], mix_qk_conv_w[i],
                                                      mix_qk_conv_b[i], mix_gate_b[i])
            hm = _mlstm(qm, km, vm, om, gt, mix_hnorm_g[i])
            hs = _stick_breaking(qs, ks, vs)
            h2 = _out_proj(h.reshape(b * t, d), hm.reshape(b * t, MV), hs.reshape(b * t, SBW),
                           mix_w_out[i], g[1])
        else:
            h2 = _conformer(h, g[0], g[1], conv_w_pw1[i], conv_b_pw1[i], conv_w_dw[i], conv_b_dw[i],
                            conv_ln_g[i], conv_ln_b[i], conv_w_pw2[i], conv_b_pw2[i]).reshape(b * t, d)
        h = _ffn(h2, g[2], g[3], ffn_w_gate[layer], ffn_w_up[layer], ffn_w_down[layer]).reshape(b, t, d)
    return h[:, N_META + PAD_FRONT:]
```

```python
import functools

import jax
import jax.numpy as jnp
from jax import lax
from jax.experimental import pallas as pl
from jax.experimental.pallas import tpu as pltpu

N_META = 16
MLSTM_HEADS = 4
MLSTM_DQK = 128
MLSTM_DV = 256
QK_CONV_WIDTH = 4
GATE_SOFTCAP = 15.0
SB_HEADS = 4
SB_DH = 128
SB_BLOCK = 128
PAD_FRONT = SB_BLOCK - N_META
CONV_WIDTH = 31
MQK = MLSTM_HEADS * MLSTM_DQK
MV = MLSTM_HEADS * MLSTM_DV
SBW = SB_HEADS * SB_DH
NEG = -1e30
EPS = 1e-6

LANES = 128
SUBLANES = 8
V7X_VMEM_BYTES = 64 * 1024 * 1024
F32_EXP_UNDERFLOW = -104.0

MLSTM_CHUNK = 128
CONV_HALO = 32
GATE_LANES = LANES

BF16 = jnp.bfloat16
F32 = jnp.float32


def _vmem_limit(block_bytes, scratch_bytes=0, temp_bytes=0):
    est = 2 * block_bytes + scratch_bytes + temp_bytes + (4 << 20)
    return int(min(max(est, 16 << 20), V7X_VMEM_BYTES - (6 << 20)))


def _nbytes(shape, dtype):
    n = 1
    for s in shape:
        n *= s
    return n * jnp.dtype(dtype).itemsize


def _seq_tile(t):
    best = SB_BLOCK
    for cand in range(SB_BLOCK, 1024 + 1, SB_BLOCK):
        if t % cand == 0:
            best = cand
    return best


def _rms(x, g):
    return x * lax.rsqrt(jnp.mean(x * x, axis=-1, keepdims=True) + EPS) * g


def _softplus(x):
    return jnp.maximum(x, 0.0) + jnp.log(1.0 + jnp.exp(-jnp.abs(x)))


def _split3(x):
    hi = x.astype(BF16)
    r = x - hi.astype(F32)
    mid = r.astype(BF16)
    lo = (r - mid.astype(F32)).astype(BF16)
    return hi, mid, lo


def _dot(a, b):
    return jnp.dot(a, b, preferred_element_type=F32)


def _dot_nt(a, b):
    return lax.dot_general(a, b, (((1,), (1,)), ((), ())), preferred_element_type=F32)


def _dot_tn(a, b):
    return lax.dot_general(a, b, (((0,), (0,)), ((), ())), preferred_element_type=F32)


def _in_proj_kernel(h_ref, g_ref, wqk_ref, wv_ref, wo_ref, wg_ref, wsq_ref, wsk_ref, wsv_ref,
                    cw_ref, cb_ref, gb_ref,
                    qm_ref, km_ref, vm_ref, om_ref, gt_ref, qs_ref, ks_ref, vs_ref, conv_scr):
    t = pl.program_id(1)
    tm = h_ref.shape[1]
    u = _rms(h_ref[0], g_ref[...])
    row = t * tm + lax.broadcasted_iota(jnp.int32, (tm, 1), 0)
    valid = row >= PAD_FRONT
    ub = jnp.where(valid, u, 0.0).astype(BF16)

    @pl.when(t == 0)
    def _():
        conv_scr[0:SUBLANES, :] = jnp.zeros((SUBLANES, 2 * MQK), F32)

    conv_scr[SUBLANES:SUBLANES + tm, :] = _dot(ub, wqk_ref[...])
    acc = cb_ref[...]
    for j in range(QK_CONV_WIDTH):
        shift = SUBLANES - (QK_CONV_WIDTH - 1) + j
        acc = acc + cw_ref[j:j + 1, :] * conv_scr[pl.ds(shift, tm), :]
    conv_scr[0:SUBLANES, :] = conv_scr[tm:tm + SUBLANES, :]
    qk = acc * jax.nn.sigmoid(acc)
    qm_ref[0] = (qk[:, :MQK] * MLSTM_DQK ** -0.5).astype(BF16)
    km_ref[0] = qk[:, MQK:].astype(BF16)

    vm_ref[0] = _dot(ub, wv_ref[...]).astype(BF16)
    om_ref[0] = _dot(ub, wo_ref[...])
    qs_ref[0] = (_dot(ub, wsq_ref[...]) * SB_DH ** -0.5).astype(BF16)
    ks_ref[0] = _dot(ub, wsk_ref[...]).astype(BF16)
    vs_ref[0] = _dot(ub, wsv_ref[...]).astype(BF16)

    gt = _dot(ub, wg_ref[...]) + gb_ref[...]
    gt = GATE_SOFTCAP * jnp.tanh(gt / GATE_SOFTCAP)
    lane = lax.broadcasted_iota(jnp.int32, (1, GATE_LANES), 1)
    log_i = jnp.where(valid, gt, NEG)
    log_f = jnp.where(valid, -_softplus(-gt), 0.0)
    gt_ref[0] = jnp.where(lane < MLSTM_HEADS, log_i, jnp.where(lane < 2 * MLSTM_HEADS, log_f, 0.0))


def _in_proj(h, g0, w_in, conv_w, conv_b, gate_b):
    b, t, d = h.shape
    tm = _seq_tile(t)
    o = [0, 2 * MQK, 2 * MQK + MV, 2 * MQK + 2 * MV, 2 * MQK + 2 * MV + 2 * MLSTM_HEADS]
    o += [o[-1] + SBW, o[-1] + 2 * SBW, o[-1] + 3 * SBW]
    wb = w_in.astype(BF16)
    wqk, wv, wo = wb[:, o[0]:o[1]], wb[:, o[1]:o[2]], wb[:, o[2]:o[3]]
    wg = jnp.pad(wb[:, o[3]:o[4]], ((0, 0), (0, GATE_LANES - 2 * MLSTM_HEADS)))
    wsq, wsk, wsv = wb[:, o[4]:o[5]], wb[:, o[5]:o[6]], wb[:, o[6]:o[7]]
    gb = jnp.pad(gate_b.astype(F32), (0, GATE_LANES - 2 * MLSTM_HEADS)).reshape(1, GATE_LANES)

    def full(a):
        return pl.BlockSpec(a.shape, lambda i, j: (0,) * a.ndim)

    def rows(width):
        return pl.BlockSpec((1, tm, width), lambda i, j: (i, j, 0))

    ins = [h, g0.reshape(1, d), wqk, wv, wo, wg, wsq, wsk, wsv, conv_w, conv_b.reshape(1, -1), gb]
    out_widths = [(MQK, BF16), (MQK, BF16), (MV, BF16), (MV, F32), (GATE_LANES, F32),
                  (SBW, BF16), (SBW, BF16), (SBW, BF16)]
    blk = _nbytes((tm, d), F32) + sum(_nbytes(a.shape, a.dtype) for a in ins[1:])
    blk += sum(_nbytes((tm, w), dt) for w, dt in out_widths)
    scr = _nbytes((tm + 2 * SUBLANES, 2 * MQK), F32)
    return pl.pallas_call(
        _in_proj_kernel,
        grid=(b, t // tm),
        in_specs=[rows(d)] + [full(a) for a in ins[1:]],
        out_specs=[rows(w) for w, _ in out_widths],
        out_shape=[jax.ShapeDtypeStruct((b, t, w), dt) for w, dt in out_widths],
        scratch_shapes=[pltpu.VMEM((tm + 2 * SUBLANES, 2 * MQK), F32)],
        compiler_params=pltpu.CompilerParams(
            dimension_semantics=("arbitrary", "arbitrary"),
            vmem_limit_bytes=_vmem_limit(blk, scr, 4 * _nbytes((tm, 2 * MQK), F32))),
        name="mixer_in_proj",
    )(*ins)


def _mlstm_kernel(q_ref, k_ref, v_ref, o_ref, gt_ref, hg_ref, out_ref, c_scr, n_scr):
    n_heads, dk, dv, ln = MLSTM_HEADS, MLSTM_DQK, MLSTM_DV, MLSTM_CHUNK
    rows = range(q_ref.shape[0])

    @pl.when(pl.program_id(1) == 0)
    def _():
        c_scr[...] = jnp.zeros(c_scr.shape, F32)
        n_scr[...] = jnp.zeros(n_scr.shape, F32)

    ri = lax.broadcasted_iota(jnp.int32, (ln, ln), 0)
    ci = lax.broadcasted_iota(jnp.int32, (ln, ln), 1)
    tril = ci <= ri
    ltri = tril.astype(BF16)
    gates, csum, gates_t, csum_t = [], [], [], []
    for bb in rows:
        g = gt_ref[bb]
        g_hi, g_mid, g_lo = _split3(g)
        cs = _dot(ltri, g_hi) + _dot(ltri, g_mid) + _dot(ltri, g_lo)
        gates.append(g)
        csum.append(cs)
        gates_t.append(g.T)
        csum_t.append(cs.T)

    chains = [(bb, hd) for bb in rows for hd in range(n_heads)]
    ids = range(len(chains))
    q = [q_ref[bb, :, hd * dk:(hd + 1) * dk] for bb, hd in chains]
    k = [k_ref[bb, :, hd * dk:(hd + 1) * dk] for bb, hd in chains]
    v = [v_ref[bb, :, hd * dv:(hd + 1) * dv] for bb, hd in chains]
    c_st = [c_scr[n] for n in ids]
    n_st = [n_scr[n:n + 1, :] for n in ids]
    bcol = [csum[bb][:, n_heads + hd:n_heads + hd + 1] for bb, hd in chains]
    s, q_c = [], []
    for n, (bb, hd) in enumerate(chains):
        brow = csum_t[bb][n_heads + hd:n_heads + hd + 1, :]
        li_row = gates_t[bb][hd:hd + 1, :]
        w_intra = jnp.exp(jnp.where(tril, bcol[n] - brow + li_row, NEG))
        s.append(_dot_nt(q[n], k[n]) * w_intra)
        q_c.append(_dot(q[n], c_st[n].astype(BF16)))
    s_v = [_dot(s[n].astype(BF16), v[n]) for n in ids]

    for n, (bb, hd) in enumerate(chains):
        w_inter = jnp.exp(bcol[n])
        num = s_v[n] + w_inter * q_c[n]
        den = (jnp.sum(s[n], axis=-1, keepdims=True)
               + w_inter * jnp.sum(q[n].astype(F32) * n_st[n], axis=-1, keepdims=True))
        h_out = num / jnp.maximum(jnp.abs(den), 1.0)
        hn = _rms(h_out, hg_ref[:, hd * dv:(hd + 1) * dv])
        gate = jax.nn.sigmoid(o_ref[bb, :, hd * dv:(hd + 1) * dv])
        out_ref[bb, :, hd * dv:(hd + 1) * dv] = (hn * gate).astype(BF16)

    for n, (bb, hd) in enumerate(chains):
        g_tot = csum[bb][ln - 1:ln, n_heads + hd:n_heads + hd + 1]
        wa = jnp.exp(g_tot - bcol[n] + gates[bb][:, hd:hd + 1])
        wc = jnp.exp(g_tot)
        kw = k[n].astype(F32) * wa
        c_scr[n] = wc * c_st[n] + _dot_tn(kw.astype(BF16), v[n])
        n_scr[n:n + 1, :] = wc * n_st[n] + jnp.sum(kw, axis=0, keepdims=True)


def _mlstm(qm, km, vm, om, gt, hnorm_g):
    b, t, _ = qm.shape
    ln = MLSTM_CHUNK
    nb = _batch_rows(b)
    assert nb * MLSTM_HEADS <= SUBLANES

    def rows(width):
        return pl.BlockSpec((nb, ln, width), lambda i, j: (i, j, 0))

    blk = nb * (2 * _nbytes((ln, MQK), BF16) + 2 * _nbytes((ln, MV), BF16) + _nbytes((ln, MV), F32)
                + _nbytes((ln, GATE_LANES), F32)) + _nbytes((1, MV), F32)
    scr = _nbytes((nb * MLSTM_HEADS, MLSTM_DQK, MLSTM_DV), F32) + _nbytes((SUBLANES, LANES), F32)
    return pl.pallas_call(
        _mlstm_kernel,
        grid=(b // nb, t // ln),
        in_specs=[rows(MQK), rows(MQK), rows(MV), rows(MV), rows(GATE_LANES),
                  pl.BlockSpec((1, MV), lambda i, j: (0, 0))],
        out_specs=rows(MV),
        out_shape=jax.ShapeDtypeStruct((b, t, MV), BF16),
        scratch_shapes=[pltpu.VMEM((nb * MLSTM_HEADS, MLSTM_DQK, MLSTM_DV), F32),
                        pltpu.VMEM((SUBLANES, LANES), F32)],
        compiler_params=pltpu.CompilerParams(
            dimension_semantics=("arbitrary", "arbitrary"),
            vmem_limit_bytes=_vmem_limit(blk, scr, 16 << 20)),
        name="mlstm_chunkwise",
    )(qm, km, vm, om, gt, hnorm_g.reshape(1, MV))


def _sb_block(q_ref, k_ref, v_ref, cmat_ref, across_scr, acc_scr, i, j, *, causal, validity, first):
    blk = SB_BLOCK
    start = pl.multiple_of(j * blk, blk)
    mask = None
    if causal or validity:
        ri = lax.broadcasted_iota(jnp.int32, (blk, blk), 0)
        s_idx = j * blk + lax.broadcasted_iota(jnp.int32, (blk, blk), 1)
        if causal:
            mask = s_idx < i * blk + ri
        if validity:
            ok = s_idx >= PAD_FRONT
            mask = ok if mask is None else jnp.logical_and(mask, ok)
    chains = [(bb, slice(hd * SB_DH, (hd + 1) * SB_DH)) for bb in range(q_ref.shape[0]) for hd in range(SB_HEADS)]
    ids = range(len(chains))
    z = [_dot_nt(q_ref[bb, :, c], k_ref[bb, pl.ds(start, blk), c]) for bb, c in chains]
    both = []
    for n in ids:
        log1m = -_softplus(z[n])
        if mask is not None:
            log1m = jnp.where(mask, log1m, 0.0)
        hi = log1m.astype(BF16)
        lo = (log1m - hi.astype(F32)).astype(BF16)
        both.append(_dot(jnp.concatenate([hi, lo], axis=1), cmat_ref[...]))
    prev = [None if first else across_scr[n] for n in ids]
    pv = []
    for n, (bb, c) in enumerate(chains):
        log_w = z[n] + both[n][:, :blk]
        if not first:
            log_w = log_w + prev[n]
        if mask is not None:
            log_w = jnp.where(mask, log_w, NEG)
        pv.append(_dot(jnp.exp(log_w).astype(BF16), v_ref[bb, pl.ds(start, blk), c]))
    amax = None
    for n, (bb, c) in enumerate(chains):
        across = both[n][:, blk:]
        if first:
            acc_scr[bb, :, c] = pv[n]
        else:
            acc_scr[bb, :, c] += pv[n]
            across = across + prev[n]
        across_scr[n] = across
        amax = across if amax is None else jnp.maximum(amax, across)
    return (jnp.max(amax) > F32_EXP_UNDERFLOW).astype(jnp.int32)


def _sb_kernel(q_ref, k_ref, v_ref, out_ref, cmat_ref, across_scr, acc_scr):
    blk = SB_BLOCK
    i = pl.program_id(1)

    @pl.when(jnp.logical_and(pl.program_id(0) == 0, i == 0))
    def _():
        ri = lax.broadcasted_iota(jnp.int32, (2 * blk, 2 * blk), 0)
        ci = lax.broadcasted_iota(jnp.int32, (2 * blk, 2 * blk), 1)
        cmat_ref[...] = jnp.logical_or(ci >= blk, (ri % blk) >= ci).astype(BF16)

    block = functools.partial(_sb_block, q_ref, k_ref, v_ref, cmat_ref, across_scr, acc_scr, i)
    go = block(i, causal=True, validity=True, first=True)

    def cond(carry):
        j, go = carry
        return jnp.logical_and(j >= 1, go > 0)

    def body(carry):
        j, _ = carry
        return j - 1, block(j, causal=False, validity=False, first=False)

    j, go = lax.while_loop(cond, body, (i - 1, go))

    @pl.when(jnp.logical_and(j == 0, go > 0))
    def _():
        block(0, causal=False, validity=True, first=False)

    out_ref[...] = acc_scr[...].astype(BF16)


def _batch_rows(b):
    return 2 if b % 2 == 0 else 1


def _stick_breaking(qs, ks, vs):
    b, t, _ = qs.shape
    blk = SB_BLOCK
    nb = _batch_rows(b)
    scr = (_nbytes((2 * blk, 2 * blk), BF16) + _nbytes((nb * SB_HEADS, blk, blk), F32)
           + _nbytes((nb, blk, SBW), F32))
    vmem = 4 * _nbytes((nb, blk, SBW), BF16) + 2 * _nbytes((nb, t, SBW), BF16) + scr + (8 << 20)
    resident = dict(pipeline_mode=pl.Buffered(1))
    return pl.pallas_call(
        _sb_kernel,
        grid=(b // nb, t // blk),
        in_specs=[pl.BlockSpec((nb, blk, SBW), lambda bi, qi: (bi, qi, 0)),
                  pl.BlockSpec((nb, t, SBW), lambda bi, qi: (bi, 0, 0), **resident),
                  pl.BlockSpec((nb, t, SBW), lambda bi, qi: (bi, 0, 0), **resident)],
        out_specs=pl.BlockSpec((nb, blk, SBW), lambda bi, qi: (bi, qi, 0)),
        out_shape=jax.ShapeDtypeStruct((b, t, SBW), BF16),
        scratch_shapes=[pltpu.VMEM((2 * blk, 2 * blk), BF16),
                        pltpu.VMEM((nb * SB_HEADS, blk, blk), F32),
                        pltpu.VMEM((nb, blk, SBW), F32)],
        compiler_params=pltpu.CompilerParams(
            dimension_semantics=("arbitrary", "arbitrary"),
            vmem_limit_bytes=int(vmem)),
        name="stick_breaking",
    )(qs, ks, vs)


def _out_proj_kernel(h_ref, hm_ref, hs_ref, wm_ref, ws_ref, g_ref, out_ref):
    y = _dot(hm_ref[...], wm_ref[...]) + _dot(hs_ref[...], ws_ref[...])
    out_ref[...] = h_ref[...] + _rms(y, g_ref[...])


def _out_proj(h2, hm2, hs2, w_out, g1):
    n, d = h2.shape
    tm = _seq_tile(n)
    wb = w_out.astype(BF16)
    wm, ws = wb[:MV], wb[MV:]
    blk = (2 * _nbytes((tm, d), F32) + _nbytes((tm, MV), BF16) + _nbytes((tm, SBW), BF16)
           + _nbytes(wb.shape, BF16) + _nbytes((1, d), F32))
    return pl.pallas_call(
        _out_proj_kernel,
        grid=(n // tm,),
        in_specs=[pl.BlockSpec((tm, d), lambda i: (i, 0)),
                  pl.BlockSpec((tm, MV), lambda i: (i, 0)),
                  pl.BlockSpec((tm, SBW), lambda i: (i, 0)),
                  pl.BlockSpec(wm.shape, lambda i: (0, 0)),
                  pl.BlockSpec(ws.shape, lambda i: (0, 0)),
                  pl.BlockSpec((1, d), lambda i: (0, 0))],
        out_specs=pl.BlockSpec((tm, d), lambda i: (i, 0)),
        out_shape=jax.ShapeDtypeStruct((n, d), F32),
        compiler_params=pltpu.CompilerParams(
            dimension_semantics=("arbitrary",),
            vmem_limit_bytes=_vmem_limit(blk, 0, 2 * _nbytes((tm, d), F32))),
        name="mixer_out_proj",
    )(h2, hm2, hs2, wm, ws, g1.reshape(1, d))


def _conformer_kernel(h_ref, g0_ref, w1_ref, b1_ref, wdw_ref, bdw_ref, lng_ref, lnb_ref,
                      w2_ref, b2_ref, g1_ref, out_ref, conv_scr):
    t = pl.program_id(1)
    tm, d = h_ref.shape[1], h_ref.shape[2]
    x = h_ref[0]
    ub = _rms(x, g0_ref[...]).astype(BF16)
    ag = _dot(ub, w1_ref[...]) + b1_ref[...]
    row = t * tm + lax.broadcasted_iota(jnp.int32, (tm, 1), 0)
    y = jnp.where(row >= PAD_FRONT, ag[:, :d] * jax.nn.sigmoid(ag[:, d:]), 0.0)

    @pl.when(t == 0)
    def _():
        conv_scr[0:CONV_HALO, :] = jnp.zeros((CONV_HALO, d), F32)

    conv_scr[CONV_HALO:CONV_HALO + tm, :] = y
    acc = None
    for r in range(SUBLANES):
        part = None
        for a in range(CONV_HALO // SUBLANES):
            lag = SUBLANES * a + r
            if lag >= CONV_WIDTH:
                continue
            j = CONV_WIDTH - 1 - lag
            term = wdw_ref[j:j + 1, :] * conv_scr[pl.ds(CONV_HALO - SUBLANES * (a + 1), tm + SUBLANES), :]
            part = term if part is None else part + term
        if r:
            part = pltpu.roll(part, r, axis=0)
        acc = part if acc is None else acc + part
    acc = acc[SUBLANES:SUBLANES + tm] + bdw_ref[...]
    conv_scr[0:CONV_HALO, :] = conv_scr[tm:tm + CONV_HALO, :]

    mu = jnp.mean(acc, axis=-1, keepdims=True)
    cen = acc - mu
    var = jnp.mean(cen * cen, axis=-1, keepdims=True)
    ln = cen * lax.rsqrt(var + EPS) * lng_ref[...] + lnb_ref[...]
    act = (ln * jax.nn.sigmoid(ln)).astype(BF16)
    z = _dot(act, w2_ref[...]) + b2_ref[...]
    out_ref[0] = x + _rms(z, g1_ref[...])


def _conformer(h, g0, g1, w_pw1, b_pw1, w_dw, b_dw, ln_g, ln_b, w_pw2, b_pw2):
    b, t, d = h.shape
    tm = _seq_tile(t)
    ins = [h, g0.reshape(1, d), w_pw1.astype(BF16), b_pw1.reshape(1, -1), w_dw, b_dw.reshape(1, d),
           ln_g.reshape(1, d), ln_b.reshape(1, d), w_pw2.astype(BF16), b_pw2.reshape(1, d), g1.reshape(1, d)]

    def full(a):
        return pl.BlockSpec(a.shape, lambda i, j: (0,) * a.ndim)

    blk = 2 * _nbytes((tm, d), F32) + sum(_nbytes(a.shape, a.dtype) for a in ins[1:])
    scr = _nbytes((tm + CONV_HALO, d), F32)
    return pl.pallas_call(
        _conformer_kernel,
        grid=(b, t // tm),
        in_specs=[pl.BlockSpec((1, tm, d), lambda i, j: (i, j, 0))] + [full(a) for a in ins[1:]],
        out_specs=pl.BlockSpec((1, tm, d), lambda i, j: (i, j, 0)),
        out_shape=jax.ShapeDtypeStruct((b, t, d), F32),
        scratch_shapes=[pltpu.VMEM((tm + CONV_HALO, d), F32)],
        compiler_params=pltpu.CompilerParams(
            dimension_semantics=("arbitrary", "arbitrary"),
            vmem_limit_bytes=_vmem_limit(blk, scr, 6 * _nbytes((tm, d), F32))),
        name="conformer_conv",
    )(*ins)


def _ffn_kernel(h_ref, g2_ref, wg_ref, wu_ref, wd_ref, g3_ref, out_ref, u_scr, acc_scr):
    c = pl.program_id(1)

    @pl.when(c == 0)
    def _():
        u_scr[...] = _rms(h_ref[...], g2_ref[...]).astype(BF16)
        acc_scr[...] = jnp.zeros(acc_scr.shape, F32)

    u = u_scr[...]
    a = _dot(u, wg_ref[...])
    hid = (a * jax.nn.sigmoid(a) * _dot(u, wu_ref[...])).astype(BF16)
    acc_scr[...] += _dot(hid, wd_ref[...])

    @pl.when(c == pl.num_programs(1) - 1)
    def _():
        out_ref[...] = h_ref[...] + _rms(acc_scr[...], g3_ref[...])


def _ffn_chunk(hidden):
    best = LANES
    for cand in range(LANES, 1408 + 1, LANES):
        if hidden % cand == 0:
            best = cand
    return best


def _ffn(h2, g2, g3, w_gate, w_up, w_down):
    n, d = h2.shape
    hidden = w_gate.shape[1]
    tm = _seq_tile(n)
    ck = _ffn_chunk(hidden)
    blk = (2 * _nbytes((tm, d), F32) + 3 * _nbytes((d, ck), BF16) + 2 * _nbytes((1, d), F32))
    scr = _nbytes((tm, d), BF16) + _nbytes((tm, d), F32)
    return pl.pallas_call(
        _ffn_kernel,
        grid=(n // tm, hidden // ck),
        in_specs=[pl.BlockSpec((tm, d), lambda i, c: (i, 0)),
                  pl.BlockSpec((1, d), lambda i, c: (0, 0)),
                  pl.BlockSpec((d, ck), lambda i, c: (0, c)),
                  pl.BlockSpec((d, ck), lambda i, c: (0, c)),
                  pl.BlockSpec((ck, d), lambda i, c: (c, 0)),
                  pl.BlockSpec((1, d), lambda i, c: (0, 0))],
        out_specs=pl.BlockSpec((tm, d), lambda i, c: (i, 0)),
        out_shape=jax.ShapeDtypeStruct((n, d), F32),
        scratch_shapes=[pltpu.VMEM((tm, d), BF16), pltpu.VMEM((tm, d), F32)],
        compiler_params=pltpu.CompilerParams(
            dimension_semantics=("arbitrary", "arbitrary"),
            vmem_limit_bytes=_vmem_limit(blk, scr, 3 * _nbytes((tm, ck), F32))),
        name="swiglu_ffn",
    )(h2, g2.reshape(1, d), w_gate.astype(BF16), w_up.astype(BF16), w_down.astype(BF16), g3.reshape(1, d))


def kernel(x, meta, norm_g, mix_w_in, mix_qk_conv_w, mix_qk_conv_b, mix_gate_b, mix_hnorm_g, mix_w_out,
           conv_w_pw1, conv_b_pw1, conv_w_dw, conv_b_dw, conv_ln_g, conv_ln_b, conv_w_pw2, conv_b_pw2,
           ffn_w_gate, ffn_w_up, ffn_w_down):
    b, seq, d = x.shape
    depth = norm_g.shape[0]
    t = seq + N_META + PAD_FRONT
    h = jnp.concatenate([jnp.zeros((b, PAD_FRONT, d), x.dtype),
                         jnp.broadcast_to(meta[None].astype(x.dtype), (b, N_META, d)), x], axis=1)
    for layer in range(depth):
        g = norm_g[layer]
        i = layer // 2
        if layer % 2 == 0:
            qm, km, vm, om, gt, qs, ks, vs = _in_proj(h, g[0], mix_w_in[i], mix_qk_conv_w[i],
                                                      mix_qk_conv_b[i], mix_gate_b[i])
            hm = _mlstm(qm, km, vm, om, gt, mix_hnorm_g[i])
            hs = _stick_breaking(qs, ks, vs)
            h2 = _out_proj(h.reshape(b * t, d), hm.reshape(b * t, MV), hs.reshape(b * t, SBW),
                           mix_w_out[i], g[1])
        else:
            h2 = _conformer(h, g[0], g[1], conv_w_pw1[i], conv_b_pw1[i], conv_w_dw[i], conv_b_dw[i],
                            conv_ln_g[i], conv_ln_b[i], conv_w_pw2[i], conv_b_pw2[i]).reshape(b * t, d)
        h = _ffn(h2, g[2], g[3], ffn_w_gate[layer], ffn_w_up[layer], ffn_w_down[layer]).reshape(b, t, d)
    return h[:, N_META + PAD_FRONT:]
```

```python
import functools

import jax
import jax.numpy as jnp
from jax import lax
from jax.experimental import pallas as pl
from jax.experimental.pallas import tpu as pltpu

N_META = 16
MLSTM_HEADS = 4
MLSTM_DQK = 128
MLSTM_DV = 256
QK_CONV_WIDTH = 4
GATE_SOFTCAP = 15.0
SB_HEADS = 4
SB_DH = 128
SB_BLOCK = 128
PAD_FRONT = SB_BLOCK - N_META
CONV_WIDTH = 31
MQK = MLSTM_HEADS * MLSTM_DQK
MV = MLSTM_HEADS * MLSTM_DV
SBW = SB_HEADS * SB_DH
NEG = -1e30
EPS = 1e-6

LANES = 128
SUBLANES = 8
V7X_VMEM_BYTES = 64 * 1024 * 1024
F32_EXP_UNDERFLOW = -104.0

MLSTM_CHUNK = 128
CONV_HALO = 32
CONV_SUBTILES = 2
GATE_LANES = LANES

BF16 = jnp.bfloat16
F32 = jnp.float32


def _vmem_limit(block_bytes, scratch_bytes=0, temp_bytes=0):
    est = 2 * block_bytes + scratch_bytes + temp_bytes + (4 << 20)
    return int(min(max(est, 16 << 20), V7X_VMEM_BYTES - (6 << 20)))


def _nbytes(shape, dtype):
    n = 1
    for s in shape:
        n *= s
    return n * jnp.dtype(dtype).itemsize


def _seq_tile(t):
    best = SB_BLOCK
    for cand in range(SB_BLOCK, 1024 + 1, SB_BLOCK):
        if t % cand == 0:
            best = cand
    return best


def _rms(x, g):
    return x * lax.rsqrt(jnp.mean(x * x, axis=-1, keepdims=True) + EPS) * g


def _softplus(x):
    return jnp.maximum(x, 0.0) + jnp.log(1.0 + jnp.exp(-jnp.abs(x)))


def _split3(x):
    hi = x.astype(BF16)
    r = x - hi.astype(F32)
    mid = r.astype(BF16)
    lo = (r - mid.astype(F32)).astype(BF16)
    return hi, mid, lo


def _dot(a, b):
    return jnp.dot(a, b, preferred_element_type=F32)


def _dot_nt(a, b):
    return lax.dot_general(a, b, (((1,), (1,)), ((), ())), preferred_element_type=F32)


def _dot_tn(a, b):
    return lax.dot_general(a, b, (((0,), (0,)), ((), ())), preferred_element_type=F32)


def _in_proj_kernel(h_ref, g_ref, wqk_ref, wv_ref, wo_ref, wg_ref, wsq_ref, wsk_ref, wsv_ref,
                    cw_ref, cb_ref, gb_ref,
                    qm_ref, km_ref, vm_ref, om_ref, gt_ref, qs_ref, ks_ref, vs_ref, conv_scr):
    t = pl.program_id(1)
    tm = h_ref.shape[1]
    u = _rms(h_ref[0], g_ref[...])
    row = t * tm + lax.broadcasted_iota(jnp.int32, (tm, 1), 0)
    valid = row >= PAD_FRONT
    ub = jnp.where(valid, u, 0.0).astype(BF16)

    @pl.when(t == 0)
    def _():
        conv_scr[0:SUBLANES, :] = jnp.zeros((SUBLANES, 2 * MQK), F32)

    conv_scr[SUBLANES:SUBLANES + tm, :] = _dot(ub, wqk_ref[...])
    acc = cb_ref[...]
    for j in range(QK_CONV_WIDTH):
        shift = SUBLANES - (QK_CONV_WIDTH - 1) + j
        acc = acc + cw_ref[j:j + 1, :] * conv_scr[pl.ds(shift, tm), :]
    conv_scr[0:SUBLANES, :] = conv_scr[tm:tm + SUBLANES, :]
    qk = acc * jax.nn.sigmoid(acc)
    qm_ref[0] = (qk[:, :MQK] * MLSTM_DQK ** -0.5).astype(BF16)
    km_ref[0] = qk[:, MQK:].astype(BF16)

    vm_ref[0] = _dot(ub, wv_ref[...]).astype(BF16)
    om_ref[0] = _dot(ub, wo_ref[...])
    qs_ref[0] = (_dot(ub, wsq_ref[...]) * SB_DH ** -0.5).astype(BF16)
    ks_ref[0] = _dot(ub, wsk_ref[...]).astype(BF16)
    vs_ref[0] = _dot(ub, wsv_ref[...]).astype(BF16)

    gt = _dot(ub, wg_ref[...]) + gb_ref[...]
    gt = GATE_SOFTCAP * jnp.tanh(gt / GATE_SOFTCAP)
    lane = lax.broadcasted_iota(jnp.int32, (1, GATE_LANES), 1)
    log_i = jnp.where(valid, gt, NEG)
    log_f = jnp.where(valid, -_softplus(-gt), 0.0)
    gt_ref[0] = jnp.where(lane < MLSTM_HEADS, log_i, jnp.where(lane < 2 * MLSTM_HEADS, log_f, 0.0))


def _in_proj(h, g0, w_in, conv_w, conv_b, gate_b):
    b, t, d = h.shape
    tm = _seq_tile(t)
    o = [0, 2 * MQK, 2 * MQK + MV, 2 * MQK + 2 * MV, 2 * MQK + 2 * MV + 2 * MLSTM_HEADS]
    o += [o[-1] + SBW, o[-1] + 2 * SBW, o[-1] + 3 * SBW]
    wb = w_in.astype(BF16)
    wqk, wv, wo = wb[:, o[0]:o[1]], wb[:, o[1]:o[2]], wb[:, o[2]:o[3]]
    wg = jnp.pad(wb[:, o[3]:o[4]], ((0, 0), (0, GATE_LANES - 2 * MLSTM_HEADS)))
    wsq, wsk, wsv = wb[:, o[4]:o[5]], wb[:, o[5]:o[6]], wb[:, o[6]:o[7]]
    gb = jnp.pad(gate_b.astype(F32), (0, GATE_LANES - 2 * MLSTM_HEADS)).reshape(1, GATE_LANES)

    def full(a):
        return pl.BlockSpec(a.shape, lambda i, j: (0,) * a.ndim)

    def rows(width):
        return pl.BlockSpec((1, tm, width), lambda i, j: (i, j, 0))

    ins = [h, g0.reshape(1, d), wqk, wv, wo, wg, wsq, wsk, wsv, conv_w, conv_b.reshape(1, -1), gb]
    out_widths = [(MQK, BF16), (MQK, BF16), (MV, BF16), (MV, F32), (GATE_LANES, F32),
                  (SBW, BF16), (SBW, BF16), (SBW, BF16)]
    blk = _nbytes((tm, d), F32) + sum(_nbytes(a.shape, a.dtype) for a in ins[1:])
    blk += sum(_nbytes((tm, w), dt) for w, dt in out_widths)
    scr = _nbytes((tm + 2 * SUBLANES, 2 * MQK), F32)
    return pl.pallas_call(
        _in_proj_kernel,
        grid=(b, t // tm),
        in_specs=[rows(d)] + [full(a) for a in ins[1:]],
        out_specs=[rows(w) for w, _ in out_widths],
        out_shape=[jax.ShapeDtypeStruct((b, t, w), dt) for w, dt in out_widths],
        scratch_shapes=[pltpu.VMEM((tm + 2 * SUBLANES, 2 * MQK), F32)],
        compiler_params=pltpu.CompilerParams(
            dimension_semantics=("arbitrary", "arbitrary"),
            vmem_limit_bytes=_vmem_limit(blk, scr, 4 * _nbytes((tm, 2 * MQK), F32))),
        name="mixer_in_proj",
    )(*ins)


def _mlstm_kernel(q_ref, k_ref, v_ref, o_ref, gt_ref, hg_ref, out_ref, c_scr, n_scr):
    n_heads, dk, dv, ln = MLSTM_HEADS, MLSTM_DQK, MLSTM_DV, MLSTM_CHUNK
    rows = range(q_ref.shape[0])

    @pl.when(pl.program_id(1) == 0)
    def _():
        c_scr[...] = jnp.zeros(c_scr.shape, F32)
        n_scr[...] = jnp.zeros(n_scr.shape, F32)

    ri = lax.broadcasted_iota(jnp.int32, (ln, ln), 0)
    ci = lax.broadcasted_iota(jnp.int32, (ln, ln), 1)
    tril = ci <= ri
    ltri = tril.astype(BF16)
    gates, csum, gates_t, csum_t = [], [], [], []
    for bb in rows:
        g = gt_ref[bb]
        g_hi, g_mid, g_lo = _split3(g)
        cs = _dot(ltri, g_hi) + _dot(ltri, g_mid) + _dot(ltri, g_lo)
        gates.append(g)
        csum.append(cs)
        gates_t.append(g.T)
        csum_t.append(cs.T)

    chains = [(bb, hd) for bb in rows for hd in range(n_heads)]
    ids = range(len(chains))
    q = [q_ref[bb, :, hd * dk:(hd + 1) * dk] for bb, hd in chains]
    k = [k_ref[bb, :, hd * dk:(hd + 1) * dk] for bb, hd in chains]
    v = [v_ref[bb, :, hd * dv:(hd + 1) * dv] for bb, hd in chains]
    c_st = [c_scr[n] for n in ids]
    n_st = [n_scr[n] for n in ids]
    bcol = [csum[bb][:, n_heads + hd:n_heads + hd + 1] for bb, hd in chains]
    ones_k = jnp.ones((ln, LANES), BF16)
    ones_v = jnp.ones((dv, LANES), BF16)
    s_b, q_c, q_n = [], [], []
    for n, (bb, hd) in enumerate(chains):
        brow = csum_t[bb][n_heads + hd:n_heads + hd + 1, :]
        li_row = gates_t[bb][hd:hd + 1, :]
        w_intra = jnp.exp(jnp.where(tril, bcol[n] - brow + li_row, NEG))
        s_b.append((_dot_nt(q[n], k[n]) * w_intra).astype(BF16))
        q_c.append(_dot(q[n], c_st[n].astype(BF16)))
        q_n.append(_dot(q[n], n_st[n].astype(BF16)))
    s_v = [_dot(s_b[n], v[n]) for n in ids]
    s_1 = [_dot(s_b[n], ones_k) for n in ids]

    for n, (bb, hd) in enumerate(chains):
        w_inter = jnp.broadcast_to(jnp.exp(bcol[n]), (ln, LANES))
        num = s_v[n] + jnp.concatenate([w_inter] * (dv // LANES), axis=1) * q_c[n]
        den = jnp.maximum(jnp.abs(s_1[n] + w_inter * q_n[n]), 1.0)
        msq = _dot((num * num).astype(BF16), ones_v) * (1.0 / dv)
        scale = lax.rsqrt(msq + EPS * den * den)
        hn = num * jnp.concatenate([scale] * (dv // LANES), axis=1) * hg_ref[:, hd * dv:(hd + 1) * dv]
        gate = jax.nn.sigmoid(o_ref[bb, :, hd * dv:(hd + 1) * dv])
        out_ref[bb, :, hd * dv:(hd + 1) * dv] = (hn * gate).astype(BF16)

    for n, (bb, hd) in enumerate(chains):
        g_tot = csum[bb][ln - 1:ln, n_heads + hd:n_heads + hd + 1]
        wa = jnp.exp(g_tot - bcol[n] + gates[bb][:, hd:hd + 1])
        wc = jnp.exp(g_tot)
        kw = (k[n].astype(F32) * wa).astype(BF16)
        c_scr[n] = wc * c_st[n] + _dot_tn(kw, v[n])
        n_scr[n] = wc * n_st[n] + _dot_tn(kw, ones_k)


def _mlstm(qm, km, vm, om, gt, hnorm_g):
    b, t, _ = qm.shape
    ln = MLSTM_CHUNK
    nb = _batch_rows(b)

    def rows(width):
        return pl.BlockSpec((nb, ln, width), lambda i, j: (i, j, 0))

    blk = nb * (2 * _nbytes((ln, MQK), BF16) + 2 * _nbytes((ln, MV), BF16) + _nbytes((ln, MV), F32)
                + _nbytes((ln, GATE_LANES), F32)) + _nbytes((1, MV), F32)
    scr = (_nbytes((nb * MLSTM_HEADS, MLSTM_DQK, MLSTM_DV), F32)
           + _nbytes((nb * MLSTM_HEADS, MLSTM_DQK, LANES), F32))
    return pl.pallas_call(
        _mlstm_kernel,
        grid=(b // nb, t // ln),
        in_specs=[rows(MQK), rows(MQK), rows(MV), rows(MV), rows(GATE_LANES),
                  pl.BlockSpec((1, MV), lambda i, j: (0, 0))],
        out_specs=rows(MV),
        out_shape=jax.ShapeDtypeStruct((b, t, MV), BF16),
        scratch_shapes=[pltpu.VMEM((nb * MLSTM_HEADS, MLSTM_DQK, MLSTM_DV), F32),
                        pltpu.VMEM((nb * MLSTM_HEADS, MLSTM_DQK, LANES), F32)],
        compiler_params=pltpu.CompilerParams(
            dimension_semantics=("arbitrary", "arbitrary"),
            vmem_limit_bytes=_vmem_limit(blk, scr, 16 << 20)),
        name="mlstm_chunkwise",
    )(qm, km, vm, om, gt, hnorm_g.reshape(1, MV))


def _sb_block(q_ref, k_ref, v_ref, cmat_ref, across_scr, acc_scr, i, j, *, causal, validity, first):
    blk = SB_BLOCK
    start = pl.multiple_of(j * blk, blk)
    mask = None
    if causal or validity:
        ri = lax.broadcasted_iota(jnp.int32, (blk, blk), 0)
        s_idx = j * blk + lax.broadcasted_iota(jnp.int32, (blk, blk), 1)
        if causal:
            mask = s_idx < i * blk + ri
        if validity:
            ok = s_idx >= PAD_FRONT
            mask = ok if mask is None else jnp.logical_and(mask, ok)
    chains = [(bb, slice(hd * SB_DH, (hd + 1) * SB_DH)) for bb in range(q_ref.shape[0]) for hd in range(SB_HEADS)]
    ids = range(len(chains))
    z = [_dot_nt(q_ref[bb, :, c], k_ref[bb, pl.ds(start, blk), c]) for bb, c in chains]
    both = []
    for n in ids:
        log1m = -_softplus(z[n])
        if mask is not None:
            log1m = jnp.where(mask, log1m, 0.0)
        hi = log1m.astype(BF16)
        lo = (log1m - hi.astype(F32)).astype(BF16)
        both.append(_dot(jnp.concatenate([hi, lo], axis=1), cmat_ref[...]))
    prev = [None if first else across_scr[n] for n in ids]
    pv = []
    for n, (bb, c) in enumerate(chains):
        log_w = z[n] + both[n][:, :blk]
        if not first:
            log_w = log_w + prev[n]
        if mask is not None:
            log_w = jnp.where(mask, log_w, NEG)
        pv.append(_dot(jnp.exp(log_w).astype(BF16), v_ref[bb, pl.ds(start, blk), c]))
    amax = None
    for n, (bb, c) in enumerate(chains):
        across = both[n][:, blk:]
        if first:
            acc_scr[bb, :, c] = pv[n]
        else:
            acc_scr[bb, :, c] += pv[n]
            across = across + prev[n]
        across_scr[n] = across
        amax = across if amax is None else jnp.maximum(amax, across)
    return (jnp.max(amax) > F32_EXP_UNDERFLOW).astype(jnp.int32)


def _sb_kernel(q_ref, k_ref, v_ref, out_ref, cmat_ref, across_scr, acc_scr):
    blk = SB_BLOCK
    i = pl.program_id(1)

    @pl.when(jnp.logical_and(pl.program_id(0) == 0, i == 0))
    def _():
        ri = lax.broadcasted_iota(jnp.int32, (2 * blk, 2 * blk), 0)
        ci = lax.broadcasted_iota(jnp.int32, (2 * blk, 2 * blk), 1)
        cmat_ref[...] = jnp.logical_or(ci >= blk, (ri % blk) >= ci).astype(BF16)

    block = functools.partial(_sb_block, q_ref, k_ref, v_ref, cmat_ref, across_scr, acc_scr, i)
    go = block(i, causal=True, validity=True, first=True)

    def cond(carry):
        j, go = carry
        return jnp.logical_and(j >= 1, go > 0)

    def body(carry):
        j, _ = carry
        return j - 1, block(j, causal=False, validity=False, first=False)

    j, go = lax.while_loop(cond, body, (i - 1, go))

    @pl.when(jnp.logical_and(j == 0, go > 0))
    def _():
        block(0, causal=False, validity=True, first=False)

    out_ref[...] = acc_scr[...].astype(BF16)


def _batch_rows(b):
    return 2 if b % 2 == 0 else 1


def _stick_breaking(qs, ks, vs):
    b, t, _ = qs.shape
    blk = SB_BLOCK
    nb = _batch_rows(b)
    scr = (_nbytes((2 * blk, 2 * blk), BF16) + _nbytes((nb * SB_HEADS, blk, blk), F32)
           + _nbytes((nb, blk, SBW), F32))
    vmem = 4 * _nbytes((nb, blk, SBW), BF16) + 2 * _nbytes((nb, t, SBW), BF16) + scr + (8 << 20)
    resident = dict(pipeline_mode=pl.Buffered(1))
    return pl.pallas_call(
        _sb_kernel,
        grid=(b // nb, t // blk),
        in_specs=[pl.BlockSpec((nb, blk, SBW), lambda bi, qi: (bi, qi, 0)),
                  pl.BlockSpec((nb, t, SBW), lambda bi, qi: (bi, 0, 0), **resident),
                  pl.BlockSpec((nb, t, SBW), lambda bi, qi: (bi, 0, 0), **resident)],
        out_specs=pl.BlockSpec((nb, blk, SBW), lambda bi, qi: (bi, qi, 0)),
        out_shape=jax.ShapeDtypeStruct((b, t, SBW), BF16),
        scratch_shapes=[pltpu.VMEM((2 * blk, 2 * blk), BF16),
                        pltpu.VMEM((nb * SB_HEADS, blk, blk), F32),
                        pltpu.VMEM((nb, blk, SBW), F32)],
        compiler_params=pltpu.CompilerParams(
            dimension_semantics=("arbitrary", "arbitrary"),
            vmem_limit_bytes=int(vmem)),
        name="stick_breaking",
    )(qs, ks, vs)


def _out_proj_kernel(h_ref, hm_ref, hs_ref, wm_ref, ws_ref, g_ref, out_ref):
    y = _dot(hm_ref[...], wm_ref[...]) + _dot(hs_ref[...], ws_ref[...])
    out_ref[...] = h_ref[...] + _rms(y, g_ref[...])


def _out_proj(h2, hm2, hs2, w_out, g1):
    n, d = h2.shape
    tm = _seq_tile(n)
    wb = w_out.astype(BF16)
    wm, ws = wb[:MV], wb[MV:]
    blk = (2 * _nbytes((tm, d), F32) + _nbytes((tm, MV), BF16) + _nbytes((tm, SBW), BF16)
           + _nbytes(wb.shape, BF16) + _nbytes((1, d), F32))
    return pl.pallas_call(
        _out_proj_kernel,
        grid=(n // tm,),
        in_specs=[pl.BlockSpec((tm, d), lambda i: (i, 0)),
                  pl.BlockSpec((tm, MV), lambda i: (i, 0)),
                  pl.BlockSpec((tm, SBW), lambda i: (i, 0)),
                  pl.BlockSpec(wm.shape, lambda i: (0, 0)),
                  pl.BlockSpec(ws.shape, lambda i: (0, 0)),
                  pl.BlockSpec((1, d), lambda i: (0, 0))],
        out_specs=pl.BlockSpec((tm, d), lambda i: (i, 0)),
        out_shape=jax.ShapeDtypeStruct((n, d), F32),
        compiler_params=pltpu.CompilerParams(
            dimension_semantics=("arbitrary",),
            vmem_limit_bytes=_vmem_limit(blk, 0, 2 * _nbytes((tm, d), F32))),
        name="mixer_out_proj",
    )(h2, hm2, hs2, wm, ws, g1.reshape(1, d))


def _conformer_kernel(h_ref, g0_ref, w1_ref, b1_ref, wdw_ref, bdw_ref, lng_ref, lnb_ref,
                      w2_ref, b2_ref, g1_ref, out_ref, conv_scr):
    t = pl.program_id(1)
    tm, d = h_ref.shape[1], h_ref.shape[2]
    ts = tm // CONV_SUBTILES
    subs = range(CONV_SUBTILES)

    @pl.when(t == 0)
    def _():
        conv_scr[0:CONV_HALO, :] = jnp.zeros((CONV_HALO, d), F32)

    for s in subs:
        ub = _rms(h_ref[0, s * ts:(s + 1) * ts], g0_ref[...]).astype(BF16)
        ag = _dot(ub, w1_ref[...]) + b1_ref[...]
        row = t * tm + s * ts + lax.broadcasted_iota(jnp.int32, (ts, 1), 0)
        y = jnp.where(row >= PAD_FRONT, ag[:, :d] * jax.nn.sigmoid(ag[:, d:]), 0.0)
        conv_scr[CONV_HALO + s * ts:CONV_HALO + (s + 1) * ts, :] = y

    conv = []
    for s in subs:
        acc = None
        for r in range(SUBLANES):
            part = None
            for a in range(CONV_HALO // SUBLANES):
                lag = SUBLANES * a + r
                if lag >= CONV_WIDTH:
                    continue
                j = CONV_WIDTH - 1 - lag
                start = CONV_HALO + s * ts - SUBLANES * (a + 1)
                term = wdw_ref[j:j + 1, :] * conv_scr[start:start + ts + SUBLANES, :]
                part = term if part is None else part + term
            if r:
                part = pltpu.roll(part, r, axis=0)
            acc = part if acc is None else acc + part
        conv.append(acc[SUBLANES:SUBLANES + ts] + bdw_ref[...])

    for s in subs:
        mu = jnp.mean(conv[s], axis=-1, keepdims=True)
        cen = conv[s] - mu
        var = jnp.mean(cen * cen, axis=-1, keepdims=True)
        ln = cen * lax.rsqrt(var + EPS) * lng_ref[...] + lnb_ref[...]
        act = (ln * jax.nn.sigmoid(ln)).astype(BF16)
        z = _dot(act, w2_ref[...]) + b2_ref[...]
        out_ref[0, s * ts:(s + 1) * ts] = h_ref[0, s * ts:(s + 1) * ts] + _rms(z, g1_ref[...])

    conv_scr[0:CONV_HALO, :] = conv_scr[tm:tm + CONV_HALO, :]


def _conformer(h, g0, g1, w_pw1, b_pw1, w_dw, b_dw, ln_g, ln_b, w_pw2, b_pw2):
    b, t, d = h.shape
    tm = _seq_tile(t)
    ins = [h, g0.reshape(1, d), w_pw1.astype(BF16), b_pw1.reshape(1, -1), w_dw, b_dw.reshape(1, d),
           ln_g.reshape(1, d), ln_b.reshape(1, d), w_pw2.astype(BF16), b_pw2.reshape(1, d), g1.reshape(1, d)]

    def full(a):
        return pl.BlockSpec(a.shape, lambda i, j: (0,) * a.ndim)

    blk = 2 * _nbytes((tm, d), F32) + sum(_nbytes(a.shape, a.dtype) for a in ins[1:])
    assert tm % (CONV_SUBTILES * CONV_HALO) == 0
    scr = _nbytes((tm + CONV_HALO, d), F32)
    return pl.pallas_call(
        _conformer_kernel,
        grid=(b, t // tm),
        in_specs=[pl.BlockSpec((1, tm, d), lambda i, j: (i, j, 0))] + [full(a) for a in ins[1:]],
        out_specs=pl.BlockSpec((1, tm, d), lambda i, j: (i, j, 0)),
        out_shape=jax.ShapeDtypeStruct((b, t, d), F32),
        scratch_shapes=[pltpu.VMEM((tm + CONV_HALO, d), F32)],
        compiler_params=pltpu.CompilerParams(
            dimension_semantics=("arbitrary", "arbitrary"),
            vmem_limit_bytes=_vmem_limit(blk, scr, 6 * _nbytes((tm, d), F32))),
        name="conformer_conv",
    )(*ins)


def _ffn_kernel(h_ref, g2_ref, wg_ref, wu_ref, wd_ref, g3_ref, out_ref, u_scr, acc_scr, *, chunk_axis):
    c = pl.program_id(chunk_axis)

    @pl.when(c == 0)
    def _():
        u_scr[...] = _rms(h_ref[...], g2_ref[...]).astype(BF16)
        acc_scr[...] = jnp.zeros(acc_scr.shape, F32)

    u = u_scr[...]
    a = _dot(u, wg_ref[...])
    hid = (a * jax.nn.sigmoid(a) * _dot(u, wu_ref[...])).astype(BF16)
    acc_scr[...] += _dot(hid, wd_ref[...])

    @pl.when(c == pl.num_programs(chunk_axis) - 1)
    def _():
        out_ref[...] = h_ref[...] + _rms(acc_scr[...], g3_ref[...])


def _ffn_chunk(hidden):
    best = LANES
    for cand in range(LANES, 1408 + 1, LANES):
        if hidden % cand == 0:
            best = cand
    return best


def _ffn(h, g2, g3, w_gate, w_up, w_down, skip=0):
    b, t, d = h.shape
    hidden = w_gate.shape[1]
    ck = _ffn_chunk(hidden)
    weights = (g2.reshape(1, d), w_gate.astype(BF16), w_up.astype(BF16), w_down.astype(BF16), g3.reshape(1, d))
    if skip == 0:
        n = b * t
        tm = _seq_tile(n)
        grid = (n // tm, hidden // ck)
        operand = h.reshape(n, d)
        h_spec = pl.BlockSpec((tm, d), lambda i, c: (i, 0))
        out_spec = pl.BlockSpec((tm, d), lambda i, c: (i, 0))
        out_shape = jax.ShapeDtypeStruct((n, d), F32)
        w_map = lambda f: (lambda i, c: f(c))
    else:
        tm = _seq_tile(t - skip)
        grid = (b, (t - skip) // tm, hidden // ck)
        operand = h
        assert skip % SUBLANES == 0 and tm % SUBLANES == 0
        h_spec = pl.BlockSpec((None, pl.Element(tm), pl.Element(d)),
                              lambda i, j, c: (i, pl.multiple_of(skip + j * tm, SUBLANES), 0))
        out_spec = pl.BlockSpec((None, tm, d), lambda i, j, c: (i, j, 0))
        out_shape = jax.ShapeDtypeStruct((b, t - skip, d), F32)
        w_map = lambda f: (lambda i, j, c: f(c))
    blk = (2 * _nbytes((tm, d), F32) + 3 * _nbytes((d, ck), BF16) + 2 * _nbytes((1, d), F32))
    scr = _nbytes((tm, d), BF16) + _nbytes((tm, d), F32)
    out = pl.pallas_call(
        functools.partial(_ffn_kernel, chunk_axis=len(grid) - 1),
        grid=grid,
        in_specs=[h_spec,
                  pl.BlockSpec((1, d), w_map(lambda c: (0, 0))),
                  pl.BlockSpec((d, ck), w_map(lambda c: (0, c))),
                  pl.BlockSpec((d, ck), w_map(lambda c: (0, c))),
                  pl.BlockSpec((ck, d), w_map(lambda c: (c, 0))),
                  pl.BlockSpec((1, d), w_map(lambda c: (0, 0)))],
        out_specs=out_spec,
        out_shape=out_shape,
        scratch_shapes=[pltpu.VMEM((tm, d), BF16), pltpu.VMEM((tm, d), F32)],
        compiler_params=pltpu.CompilerParams(
            dimension_semantics=("arbitrary",) * len(grid),
            vmem_limit_bytes=_vmem_limit(blk, scr, 3 * _nbytes((tm, ck), F32))),
        name="swiglu_ffn",
    )(operand, *weights)
    return out if skip else out.reshape(b, t, d)


def kernel(x, meta, norm_g, mix_w_in, mix_qk_conv_w, mix_qk_conv_b, mix_gate_b, mix_hnorm_g, mix_w_out,
           conv_w_pw1, conv_b_pw1, conv_w_dw, conv_b_dw, conv_ln_g, conv_ln_b, conv_w_pw2, conv_b_pw2,
           ffn_w_gate, ffn_w_up, ffn_w_down):
    b, seq, d = x.shape
    depth = norm_g.shape[0]
    t = seq + N_META + PAD_FRONT
    h = jnp.concatenate([jnp.zeros((b, PAD_FRONT, d), x.dtype),
                         jnp.broadcast_to(meta[None].astype(x.dtype), (b, N_META, d)), x], axis=1)
    for layer in range(depth):
        g = norm_g[layer]
        i = layer // 2
        if layer % 2 == 0:
            qm, km, vm, om, gt, qs, ks, vs = _in_proj(h, g[0], mix_w_in[i], mix_qk_conv_w[i],
                                                      mix_qk_conv_b[i], mix_gate_b[i])
            hm = _mlstm(qm, km, vm, om, gt, mix_hnorm_g[i])
            hs = _stick_breaking(qs, ks, vs)
            h1 = _out_proj(h.reshape(b * t, d), hm.reshape(b * t, MV), hs.reshape(b * t, SBW),
                           mix_w_out[i], g[1]).reshape(b, t, d)
        else:
            h1 = _conformer(h, g[0], g[1], conv_w_pw1[i], conv_b_pw1[i], conv_w_dw[i], conv_b_dw[i],
                            conv_ln_g[i], conv_ln_b[i], conv_w_pw2[i], conv_b_pw2[i])
        skip = N_META + PAD_FRONT if layer == depth - 1 else 0
        h = _ffn(h1, g[2], g[3], ffn_w_gate[layer], ffn_w_up[layer], ffn_w_down[layer], skip=skip)
    return h
```

```python
import functools

import jax
import jax.numpy as jnp
from jax import lax
from jax.experimental import pallas as pl
from jax.experimental.pallas import tpu as pltpu

N_META = 16
MLSTM_HEADS = 4
MLSTM_DQK = 128
MLSTM_DV = 256
QK_CONV_WIDTH = 4
GATE_SOFTCAP = 15.0
SB_HEADS = 4
SB_DH = 128
SB_BLOCK = 128
PAD_FRONT = SB_BLOCK - N_META
CONV_WIDTH = 31
MQK = MLSTM_HEADS * MLSTM_DQK
MV = MLSTM_HEADS * MLSTM_DV
SBW = SB_HEADS * SB_DH
NEG = -1e30
EPS = 1e-6

LANES = 128
SUBLANES = 8
V7X_VMEM_BYTES = 64 * 1024 * 1024
F32_EXP_UNDERFLOW = -104.0

MLSTM_CHUNK = 128
CONV_HALO = 32
CONV_SUBTILES = 2
GATE_LANES = LANES

BF16 = jnp.bfloat16
F32 = jnp.float32


def _vmem_limit(block_bytes, scratch_bytes=0, temp_bytes=0):
    est = 2 * block_bytes + scratch_bytes + temp_bytes + (4 << 20)
    return int(min(max(est, 16 << 20), V7X_VMEM_BYTES - (6 << 20)))


def _nbytes(shape, dtype):
    n = 1
    for s in shape:
        n *= s
    return n * jnp.dtype(dtype).itemsize


def _seq_tile(t, cap=1024):
    best = SB_BLOCK
    for cand in range(SB_BLOCK, cap + 1, SB_BLOCK):
        if t % cand == 0:
            best = cand
    return best


def _rms(x, g):
    return x * lax.rsqrt(jnp.mean(x * x, axis=-1, keepdims=True) + EPS) * g


def _softplus(x):
    return jnp.maximum(x, 0.0) + jnp.log(1.0 + jnp.exp(-jnp.abs(x)))


def _split3(x):
    hi = x.astype(BF16)
    r = x - hi.astype(F32)
    mid = r.astype(BF16)
    lo = (r - mid.astype(F32)).astype(BF16)
    return hi, mid, lo


def _dot(a, b):
    return jnp.dot(a, b, preferred_element_type=F32)


def _dot_nt(a, b):
    return lax.dot_general(a, b, (((1,), (1,)), ((), ())), preferred_element_type=F32)


def _dot_tn(a, b):
    return lax.dot_general(a, b, (((0,), (0,)), ((), ())), preferred_element_type=F32)


def _in_proj_kernel(h_ref, g_ref, wqk_ref, wv_ref, wo_ref, wg_ref, wsq_ref, wsk_ref, wsv_ref,
                    cw_ref, cb_ref, gb_ref,
                    qm_ref, km_ref, vm_ref, om_ref, gt_ref, qs_ref, ks_ref, vs_ref, conv_scr):
    t = pl.program_id(1)
    tm = h_ref.shape[1]
    u = _rms(h_ref[0], g_ref[...])
    row = t * tm + lax.broadcasted_iota(jnp.int32, (tm, 1), 0)
    valid = row >= PAD_FRONT
    ub = jnp.where(valid, u, 0.0).astype(BF16)

    @pl.when(t == 0)
    def _():
        conv_scr[0:SUBLANES, :] = jnp.zeros((SUBLANES, 2 * MQK), F32)

    conv_scr[SUBLANES:SUBLANES + tm, :] = _dot(ub, wqk_ref[...])
    acc = cb_ref[...]
    for j in range(QK_CONV_WIDTH):
        shift = SUBLANES - (QK_CONV_WIDTH - 1) + j
        acc = acc + cw_ref[j:j + 1, :] * conv_scr[pl.ds(shift, tm), :]
    conv_scr[0:SUBLANES, :] = conv_scr[tm:tm + SUBLANES, :]
    qk = acc * jax.nn.sigmoid(acc)
    qm_ref[0] = (qk[:, :MQK] * MLSTM_DQK ** -0.5).astype(BF16)
    km_ref[0] = qk[:, MQK:].astype(BF16)

    vm_ref[0] = _dot(ub, wv_ref[...]).astype(BF16)
    om_ref[0] = _dot(ub, wo_ref[...])
    qs_ref[0] = (_dot(ub, wsq_ref[...]) * SB_DH ** -0.5).astype(BF16)
    ks_ref[0] = _dot(ub, wsk_ref[...]).astype(BF16)
    vs_ref[0] = _dot(ub, wsv_ref[...]).astype(BF16)

    gt = _dot(ub, wg_ref[...]) + gb_ref[...]
    gt = GATE_SOFTCAP * jnp.tanh(gt / GATE_SOFTCAP)
    lane = lax.broadcasted_iota(jnp.int32, (1, GATE_LANES), 1)
    log_i = jnp.where(valid, gt, NEG)
    log_f = jnp.where(valid, -_softplus(-gt), 0.0)
    gt_ref[0] = jnp.where(lane < MLSTM_HEADS, log_i, jnp.where(lane < 2 * MLSTM_HEADS, log_f, 0.0))


def _in_proj(h, g0, w_in, conv_w, conv_b, gate_b):
    b, t, d = h.shape
    tm = _seq_tile(t)
    o = [0, 2 * MQK, 2 * MQK + MV, 2 * MQK + 2 * MV, 2 * MQK + 2 * MV + 2 * MLSTM_HEADS]
    o += [o[-1] + SBW, o[-1] + 2 * SBW, o[-1] + 3 * SBW]
    wb = w_in.astype(BF16)
    wqk, wv, wo = wb[:, o[0]:o[1]], wb[:, o[1]:o[2]], wb[:, o[2]:o[3]]
    wg = jnp.pad(wb[:, o[3]:o[4]], ((0, 0), (0, GATE_LANES - 2 * MLSTM_HEADS)))
    wsq, wsk, wsv = wb[:, o[4]:o[5]], wb[:, o[5]:o[6]], wb[:, o[6]:o[7]]
    gb = jnp.pad(gate_b.astype(F32), (0, GATE_LANES - 2 * MLSTM_HEADS)).reshape(1, GATE_LANES)

    def full(a):
        return pl.BlockSpec(a.shape, lambda i, j: (0,) * a.ndim)

    def rows(width):
        return pl.BlockSpec((1, tm, width), lambda i, j: (i, j, 0))

    ins = [h, g0.reshape(1, d), wqk, wv, wo, wg, wsq, wsk, wsv, conv_w, conv_b.reshape(1, -1), gb]
    out_widths = [(MQK, BF16), (MQK, BF16), (MV, BF16), (MV, F32), (GATE_LANES, F32),
                  (SBW, BF16), (SBW, BF16), (SBW, BF16)]
    blk = _nbytes((tm, d), F32) + sum(_nbytes(a.shape, a.dtype) for a in ins[1:])
    blk += sum(_nbytes((tm, w), dt) for w, dt in out_widths)
    scr = _nbytes((tm + 2 * SUBLANES, 2 * MQK), F32)
    return pl.pallas_call(
        _in_proj_kernel,
        grid=(b, t // tm),
        in_specs=[rows(d)] + [full(a) for a in ins[1:]],
        out_specs=[rows(w) for w, _ in out_widths],
        out_shape=[jax.ShapeDtypeStruct((b, t, w), dt) for w, dt in out_widths],
        scratch_shapes=[pltpu.VMEM((tm + 2 * SUBLANES, 2 * MQK), F32)],
        compiler_params=pltpu.CompilerParams(
            dimension_semantics=("arbitrary", "arbitrary"),
            vmem_limit_bytes=_vmem_limit(blk, scr, 4 * _nbytes((tm, 2 * MQK), F32))),
        name="mixer_in_proj",
    )(*ins)


def _mlstm_kernel(q_ref, k_ref, v_ref, o_ref, gt_ref, hg_ref, out_ref, c_scr, n_scr):
    n_heads, dk, dv, ln = MLSTM_HEADS, MLSTM_DQK, MLSTM_DV, MLSTM_CHUNK
    rows = range(q_ref.shape[0])

    @pl.when(pl.program_id(1) == 0)
    def _():
        c_scr[...] = jnp.zeros(c_scr.shape, F32)
        n_scr[...] = jnp.zeros(n_scr.shape, F32)

    ri = lax.broadcasted_iota(jnp.int32, (ln, ln), 0)
    ci = lax.broadcasted_iota(jnp.int32, (ln, ln), 1)
    tril = ci <= ri
    ltri = tril.astype(BF16)
    gates, csum, gates_t, csum_t = [], [], [], []
    for bb in rows:
        g = gt_ref[bb]
        g_hi, g_mid, g_lo = _split3(g)
        cs = _dot(ltri, g_hi) + _dot(ltri, g_mid) + _dot(ltri, g_lo)
        gates.append(g)
        csum.append(cs)
        gates_t.append(g.T)
        csum_t.append(cs.T)

    chains = [(bb, hd) for bb in rows for hd in range(n_heads)]
    ids = range(len(chains))
    q = [q_ref[bb, :, hd * dk:(hd + 1) * dk] for bb, hd in chains]
    k = [k_ref[bb, :, hd * dk:(hd + 1) * dk] for bb, hd in chains]
    v = [v_ref[bb, :, hd * dv:(hd + 1) * dv] for bb, hd in chains]
    c_st = [c_scr[n] for n in ids]
    n_st = [n_scr[n] for n in ids]
    bcol = [csum[bb][:, n_heads + hd:n_heads + hd + 1] for bb, hd in chains]
    ones_k = jnp.ones((ln, LANES), BF16)
    ones_v = jnp.ones((dv, LANES), BF16)
    s_b, q_c, q_n = [], [], []
    for n, (bb, hd) in enumerate(chains):
        brow = csum_t[bb][n_heads + hd:n_heads + hd + 1, :]
        li_row = gates_t[bb][hd:hd + 1, :]
        w_intra = jnp.exp(jnp.where(tril, bcol[n] - brow + li_row, NEG))
        s_b.append((_dot_nt(q[n], k[n]) * w_intra).astype(BF16))
        q_c.append(_dot(q[n], c_st[n].astype(BF16)))
        q_n.append(_dot(q[n], n_st[n].astype(BF16)))
    s_v = [_dot(s_b[n], v[n]) for n in ids]
    s_1 = [_dot(s_b[n], ones_k) for n in ids]

    for n, (bb, hd) in enumerate(chains):
        w_inter = jnp.broadcast_to(jnp.exp(bcol[n]), (ln, LANES))
        num = s_v[n] + jnp.concatenate([w_inter] * (dv // LANES), axis=1) * q_c[n]
        den = jnp.maximum(jnp.abs(s_1[n] + w_inter * q_n[n]), 1.0)
        msq = _dot((num * num).astype(BF16), ones_v) * (1.0 / dv)
        scale = lax.rsqrt(msq + EPS * den * den)
        hn = num * jnp.concatenate([scale] * (dv // LANES), axis=1) * hg_ref[:, hd * dv:(hd + 1) * dv]
        gate = jax.nn.sigmoid(o_ref[bb, :, hd * dv:(hd + 1) * dv])
        out_ref[bb, :, hd * dv:(hd + 1) * dv] = (hn * gate).astype(BF16)

    for n, (bb, hd) in enumerate(chains):
        g_tot = csum[bb][ln - 1:ln, n_heads + hd:n_heads + hd + 1]
        wa = jnp.exp(g_tot - bcol[n] + gates[bb][:, hd:hd + 1])
        wc = jnp.exp(g_tot)
        kw = (k[n].astype(F32) * wa).astype(BF16)
        c_scr[n] = wc * c_st[n] + _dot_tn(kw, v[n])
        n_scr[n] = wc * n_st[n] + _dot_tn(kw, ones_k)


def _mlstm(qm, km, vm, om, gt, hnorm_g):
    b, t, _ = qm.shape
    ln = MLSTM_CHUNK
    nb = _batch_rows(b)

    def rows(width):
        return pl.BlockSpec((nb, ln, width), lambda i, j: (i, j, 0))

    blk = nb * (2 * _nbytes((ln, MQK), BF16) + 2 * _nbytes((ln, MV), BF16) + _nbytes((ln, MV), F32)
                + _nbytes((ln, GATE_LANES), F32)) + _nbytes((1, MV), F32)
    scr = (_nbytes((nb * MLSTM_HEADS, MLSTM_DQK, MLSTM_DV), F32)
           + _nbytes((nb * MLSTM_HEADS, MLSTM_DQK, LANES), F32))
    return pl.pallas_call(
        _mlstm_kernel,
        grid=(b // nb, t // ln),
        in_specs=[rows(MQK), rows(MQK), rows(MV), rows(MV), rows(GATE_LANES),
                  pl.BlockSpec((1, MV), lambda i, j: (0, 0))],
        out_specs=rows(MV),
        out_shape=jax.ShapeDtypeStruct((b, t, MV), BF16),
        scratch_shapes=[pltpu.VMEM((nb * MLSTM_HEADS, MLSTM_DQK, MLSTM_DV), F32),
                        pltpu.VMEM((nb * MLSTM_HEADS, MLSTM_DQK, LANES), F32)],
        compiler_params=pltpu.CompilerParams(
            dimension_semantics=("arbitrary", "arbitrary"),
            vmem_limit_bytes=_vmem_limit(blk, scr, 16 << 20)),
        name="mlstm_chunkwise",
    )(qm, km, vm, om, gt, hnorm_g.reshape(1, MV))


def _sb_block(q_ref, k_ref, v_ref, cmat_ref, across_scr, acc_scr, i, j, *, causal, validity, first):
    blk = SB_BLOCK
    start = pl.multiple_of(j * blk, blk)
    mask = None
    if causal or validity:
        ri = lax.broadcasted_iota(jnp.int32, (blk, blk), 0)
        s_idx = j * blk + lax.broadcasted_iota(jnp.int32, (blk, blk), 1)
        if causal:
            mask = s_idx < i * blk + ri
        if validity:
            ok = s_idx >= PAD_FRONT
            mask = ok if mask is None else jnp.logical_and(mask, ok)
    chains = [(bb, slice(hd * SB_DH, (hd + 1) * SB_DH)) for bb in range(q_ref.shape[0]) for hd in range(SB_HEADS)]
    ids = range(len(chains))
    z = [_dot_nt(q_ref[bb, :, c], k_ref[bb, pl.ds(start, blk), c]) for bb, c in chains]
    both = []
    for n in ids:
        log1m = -_softplus(z[n])
        if mask is not None:
            log1m = jnp.where(mask, log1m, 0.0)
        hi = log1m.astype(BF16)
        lo = (log1m - hi.astype(F32)).astype(BF16)
        both.append(_dot(jnp.concatenate([hi, lo], axis=1), cmat_ref[...]))
    prev = [None if first else across_scr[n] for n in ids]
    pv = []
    for n, (bb, c) in enumerate(chains):
        log_w = z[n] + both[n][:, :blk]
        if not first:
            log_w = log_w + prev[n]
        if mask is not None:
            log_w = jnp.where(mask, log_w, NEG)
        pv.append(_dot(jnp.exp(log_w).astype(BF16), v_ref[bb, pl.ds(start, blk), c]))
    amax = None
    for n, (bb, c) in enumerate(chains):
        across = both[n][:, blk:]
        if first:
            acc_scr[bb, :, c] = pv[n]
        else:
            acc_scr[bb, :, c] += pv[n]
            across = across + prev[n]
        across_scr[n] = across
        amax = across if amax is None else jnp.maximum(amax, across)
    return (jnp.max(amax) > F32_EXP_UNDERFLOW).astype(jnp.int32)


def _sb_kernel(q_ref, k_ref, v_ref, out_ref, cmat_ref, across_scr, acc_scr):
    blk = SB_BLOCK
    i = pl.program_id(1)

    @pl.when(jnp.logical_and(pl.program_id(0) == 0, i == 0))
    def _():
        ri = lax.broadcasted_iota(jnp.int32, (2 * blk, 2 * blk), 0)
        ci = lax.broadcasted_iota(jnp.int32, (2 * blk, 2 * blk), 1)
        cmat_ref[...] = jnp.logical_or(ci >= blk, (ri % blk) >= ci).astype(BF16)

    block = functools.partial(_sb_block, q_ref, k_ref, v_ref, cmat_ref, across_scr, acc_scr, i)
    go = block(i, causal=True, validity=True, first=True)

    def cond(carry):
        j, go = carry
        return jnp.logical_and(j >= 1, go > 0)

    def body(carry):
        j, _ = carry
        return j - 1, block(j, causal=False, validity=False, first=False)

    j, go = lax.while_loop(cond, body, (i - 1, go))

    @pl.when(jnp.logical_and(j == 0, go > 0))
    def _():
        block(0, causal=False, validity=True, first=False)

    out_ref[...] = acc_scr[...].astype(BF16)


def _batch_rows(b):
    return 2 if b % 2 == 0 else 1


def _stick_breaking(qs, ks, vs):
    b, t, _ = qs.shape
    blk = SB_BLOCK
    nb = _batch_rows(b)
    scr = (_nbytes((2 * blk, 2 * blk), BF16) + _nbytes((nb * SB_HEADS, blk, blk), F32)
           + _nbytes((nb, blk, SBW), F32))
    vmem = 4 * _nbytes((nb, blk, SBW), BF16) + 2 * _nbytes((nb, t, SBW), BF16) + scr + (8 << 20)
    resident = dict(pipeline_mode=pl.Buffered(1))
    return pl.pallas_call(
        _sb_kernel,
        grid=(b // nb, t // blk),
        in_specs=[pl.BlockSpec((nb, blk, SBW), lambda bi, qi: (bi, qi, 0)),
                  pl.BlockSpec((nb, t, SBW), lambda bi, qi: (bi, 0, 0), **resident),
                  pl.BlockSpec((nb, t, SBW), lambda bi, qi: (bi, 0, 0), **resident)],
        out_specs=pl.BlockSpec((nb, blk, SBW), lambda bi, qi: (bi, qi, 0)),
        out_shape=jax.ShapeDtypeStruct((b, t, SBW), BF16),
        scratch_shapes=[pltpu.VMEM((2 * blk, 2 * blk), BF16),
                        pltpu.VMEM((nb * SB_HEADS, blk, blk), F32),
                        pltpu.VMEM((nb, blk, SBW), F32)],
        compiler_params=pltpu.CompilerParams(
            dimension_semantics=("arbitrary", "arbitrary"),
            vmem_limit_bytes=int(vmem)),
        name="stick_breaking",
    )(qs, ks, vs)


def _out_proj_kernel(h_ref, hm_ref, hs_ref, wm_ref, ws_ref, g_ref, out_ref):
    y = _dot(hm_ref[...], wm_ref[...]) + _dot(hs_ref[...], ws_ref[...])
    out_ref[...] = h_ref[...] + _rms(y, g_ref[...])


def _out_proj(h2, hm2, hs2, w_out, g1):
    n, d = h2.shape
    tm = _seq_tile(n)
    wb = w_out.astype(BF16)
    wm, ws = wb[:MV], wb[MV:]
    blk = (2 * _nbytes((tm, d), F32) + _nbytes((tm, MV), BF16) + _nbytes((tm, SBW), BF16)
           + _nbytes(wb.shape, BF16) + _nbytes((1, d), F32))
    return pl.pallas_call(
        _out_proj_kernel,
        grid=(n // tm,),
        in_specs=[pl.BlockSpec((tm, d), lambda i: (i, 0)),
                  pl.BlockSpec((tm, MV), lambda i: (i, 0)),
                  pl.BlockSpec((tm, SBW), lambda i: (i, 0)),
                  pl.BlockSpec(wm.shape, lambda i: (0, 0)),
                  pl.BlockSpec(ws.shape, lambda i: (0, 0)),
                  pl.BlockSpec((1, d), lambda i: (0, 0))],
        out_specs=pl.BlockSpec((tm, d), lambda i: (i, 0)),
        out_shape=jax.ShapeDtypeStruct((n, d), F32),
        compiler_params=pltpu.CompilerParams(
            dimension_semantics=("arbitrary",),
            vmem_limit_bytes=_vmem_limit(blk, 0, 2 * _nbytes((tm, d), F32))),
        name="mixer_out_proj",
    )(h2, hm2, hs2, wm, ws, g1.reshape(1, d))


def _conformer_kernel(h_ref, g0_ref, w1_ref, b1_ref, wdw_ref, bdw_ref, lng_ref, lnb_ref,
                      w2_ref, b2_ref, g1_ref, out_ref, conv_scr):
    t = pl.program_id(1)
    tm, d = h_ref.shape[1], h_ref.shape[2]
    ts = tm // CONV_SUBTILES
    subs = range(CONV_SUBTILES)

    @pl.when(t == 0)
    def _():
        conv_scr[0:CONV_HALO, :] = jnp.zeros((CONV_HALO, d), F32)

    for s in subs:
        ub = _rms(h_ref[0, s * ts:(s + 1) * ts], g0_ref[...]).astype(BF16)
        ag = _dot(ub, w1_ref[...]) + b1_ref[...]
        row = t * tm + s * ts + lax.broadcasted_iota(jnp.int32, (ts, 1), 0)
        y = jnp.where(row >= PAD_FRONT, ag[:, :d] * jax.nn.sigmoid(ag[:, d:]), 0.0)
        conv_scr[CONV_HALO + s * ts:CONV_HALO + (s + 1) * ts, :] = y

    conv = []
    for s in subs:
        acc = None
        for r in range(SUBLANES):
            part = None
            for a in range(CONV_HALO // SUBLANES):
                lag = SUBLANES * a + r
                if lag >= CONV_WIDTH:
                    continue
                j = CONV_WIDTH - 1 - lag
                start = CONV_HALO + s * ts - SUBLANES * (a + 1)
                term = wdw_ref[j:j + 1, :] * conv_scr[start:start + ts + SUBLANES, :]
                part = term if part is None else part + term
            if r:
                part = pltpu.roll(part, r, axis=0)
            acc = part if acc is None else acc + part
        conv.append(acc[SUBLANES:SUBLANES + ts] + bdw_ref[...])

    for s in subs:
        mu = jnp.mean(conv[s], axis=-1, keepdims=True)
        cen = conv[s] - mu
        var = jnp.mean(cen * cen, axis=-1, keepdims=True)
        ln = cen * lax.rsqrt(var + EPS) * lng_ref[...] + lnb_ref[...]
        act = (ln * jax.nn.sigmoid(ln)).astype(BF16)
        z = _dot(act, w2_ref[...]) + b2_ref[...]
        out_ref[0, s * ts:(s + 1) * ts] = h_ref[0, s * ts:(s + 1) * ts] + _rms(z, g1_ref[...])

    conv_scr[0:CONV_HALO, :] = conv_scr[tm:tm + CONV_HALO, :]


def _conformer(h, g0, g1, w_pw1, b_pw1, w_dw, b_dw, ln_g, ln_b, w_pw2, b_pw2):
    b, t, d = h.shape
    tm = _seq_tile(t)
    ins = [h, g0.reshape(1, d), w_pw1.astype(BF16), b_pw1.reshape(1, -1), w_dw, b_dw.reshape(1, d),
           ln_g.reshape(1, d), ln_b.reshape(1, d), w_pw2.astype(BF16), b_pw2.reshape(1, d), g1.reshape(1, d)]

    def full(a):
        return pl.BlockSpec(a.shape, lambda i, j: (0,) * a.ndim)

    blk = 2 * _nbytes((tm, d), F32) + sum(_nbytes(a.shape, a.dtype) for a in ins[1:])
    assert tm % (CONV_SUBTILES * CONV_HALO) == 0
    scr = _nbytes((tm + CONV_HALO, d), F32)
    return pl.pallas_call(
        _conformer_kernel,
        grid=(b, t // tm),
        in_specs=[pl.BlockSpec((1, tm, d), lambda i, j: (i, j, 0))] + [full(a) for a in ins[1:]],
        out_specs=pl.BlockSpec((1, tm, d), lambda i, j: (i, j, 0)),
        out_shape=jax.ShapeDtypeStruct((b, t, d), F32),
        scratch_shapes=[pltpu.VMEM((tm + CONV_HALO, d), F32)],
        compiler_params=pltpu.CompilerParams(
            dimension_semantics=("arbitrary", "arbitrary"),
            vmem_limit_bytes=_vmem_limit(blk, scr, 6 * _nbytes((tm, d), F32))),
        name="conformer_conv",
    )(*ins)


def _ffn_kernel(h_ref, res_ref, g2_ref, wg_ref, wu_ref, wd_ref, g3_ref, out_ref,
                u_even, u_odd, acc_even, acc_odd, *, chunks):
    i = pl.program_id(0)

    @pl.when(i == 0)
    def _():
        for ref in (u_even, u_odd, acc_even, acc_odd):
            ref[...] = jnp.zeros(ref.shape, ref.dtype)

    def step(u_new, u_prev, acc_new, acc_prev):
        out_ref[...] = res_ref[...] + _rms(acc_prev[...], g3_ref[...])
        u_new[...] = _rms(h_ref[...], g2_ref[...]).astype(BF16)
        u = u_prev[...]
        acc = None
        for lo, hi in chunks:
            a = _dot(u, wg_ref[:, lo:hi])
            hid = (a * jax.nn.sigmoid(a) * _dot(u, wu_ref[:, lo:hi])).astype(BF16)
            part = _dot(hid, wd_ref[lo:hi, :])
            acc = part if acc is None else acc + part
        acc_new[...] = acc

    @pl.when(i % 2 == 0)
    def _():
        step(u_even, u_odd, acc_odd, acc_even)

    @pl.when(i % 2 == 1)
    def _():
        step(u_odd, u_even, acc_even, acc_odd)


FFN_PIPELINE_LAG = 2
MXU_DIM = 256


def _ffn_chunks(hidden):
    if hidden % MXU_DIM:
        return ((0, hidden),)
    units = hidden // MXU_DIM
    pieces = -(-units // 6)
    bounds = [round(units * p / pieces) * MXU_DIM for p in range(pieces + 1)]
    return tuple(zip(bounds[:-1], bounds[1:]))


def _ffn(h, g2, g3, w_gate, w_up, w_down, skip=0):
    b, t, d = h.shape
    hidden = w_gate.shape[1]
    chunks = _ffn_chunks(hidden)
    weights = (g2.reshape(1, d), w_gate.astype(BF16), w_up.astype(BF16), w_down.astype(BF16), g3.reshape(1, d))
    lag = FFN_PIPELINE_LAG
    if skip == 0:
        rows = b * t
        tm = _seq_tile(rows, cap=512)
        n_tiles = rows // tm
        operand = h.reshape(rows, d)
        in_tile = lambda i: (jnp.minimum(i, n_tiles - 1), 0)
        out_tile = lambda i: (jnp.maximum(i - lag, 0), 0)
        h_spec = pl.BlockSpec((tm, d), in_tile)
        res_spec = pl.BlockSpec((tm, d), out_tile)
        out_spec = pl.BlockSpec((tm, d), out_tile)
        out_shape = jax.ShapeDtypeStruct((rows, d), F32)
    else:
        tm = _seq_tile(t - skip, cap=512)
        per_row = (t - skip) // tm
        n_tiles = b * per_row
        operand = h
        assert skip % SUBLANES == 0

        def rows_at(tile):
            return (tile // per_row, pl.multiple_of(skip + (tile % per_row) * tm, SUBLANES), 0)

        def out_tile(i):
            tile = jnp.maximum(i - lag, 0)
            return (tile // per_row, tile % per_row, 0)

        elem = (None, pl.Element(tm), pl.Element(d))
        h_spec = pl.BlockSpec(elem, lambda i: rows_at(jnp.minimum(i, n_tiles - 1)))
        res_spec = pl.BlockSpec(elem, lambda i: rows_at(jnp.maximum(i - lag, 0)))
        out_spec = pl.BlockSpec((None, tm, d), out_tile)
        out_shape = jax.ShapeDtypeStruct((b, t - skip, d), F32)

    def resident(a):
        return pl.BlockSpec(a.shape, lambda i: (0,) * a.ndim, pipeline_mode=pl.Buffered(1))

    widest = max(hi - lo for lo, hi in chunks)
    vmem = (6 * _nbytes((tm, d), F32) + sum(_nbytes(a.shape, a.dtype) for a in weights)
            + 2 * _nbytes((tm, d), BF16) + 2 * _nbytes((tm, d), F32)
            + len(chunks) * 3 * _nbytes((tm, widest), F32) + 2 * _nbytes((tm, d), F32) + (4 << 20))
    out = pl.pallas_call(
        functools.partial(_ffn_kernel, chunks=chunks),
        grid=(n_tiles + lag,),
        in_specs=[h_spec, res_spec] + [resident(a) for a in weights],
        out_specs=out_spec,
        out_shape=out_shape,
        scratch_shapes=[pltpu.VMEM((tm, d), BF16), pltpu.VMEM((tm, d), BF16),
                        pltpu.VMEM((tm, d), F32), pltpu.VMEM((tm, d), F32)],
        compiler_params=pltpu.CompilerParams(
            dimension_semantics=("arbitrary",),
            vmem_limit_bytes=int(min(vmem, V7X_VMEM_BYTES - (6 << 20)))),
        name="swiglu_ffn",
    )(operand, operand, *weights)
    return out if skip else out.reshape(b, t, d)


def kernel(x, meta, norm_g, mix_w_in, mix_qk_conv_w, mix_qk_conv_b, mix_gate_b, mix_hnorm_g, mix_w_out,
           conv_w_pw1, conv_b_pw1, conv_w_dw, conv_b_dw, conv_ln_g, conv_ln_b, conv_w_pw2, conv_b_pw2,
           ffn_w_gate, ffn_w_up, ffn_w_down):
    b, seq, d = x.shape
    depth = norm_g.shape[0]
    t = seq + N_META + PAD_FRONT
    h = jnp.concatenate([jnp.zeros((b, PAD_FRONT, d), x.dtype),
                         jnp.broadcast_to(meta[None].astype(x.dtype), (b, N_META, d)), x], axis=1)
    for layer in range(depth):
        g = norm_g[layer]
        i = layer // 2
        if layer % 2 == 0:
            qm, km, vm, om, gt, qs, ks, vs = _in_proj(h, g[0], mix_w_in[i], mix_qk_conv_w[i],
                                                      mix_qk_conv_b[i], mix_gate_b[i])
            hm = _mlstm(qm, km, vm, om, gt, mix_hnorm_g[i])
            hs = _stick_breaking(qs, ks, vs)
            h1 = _out_proj(h.reshape(b * t, d), hm.reshape(b * t, MV), hs.reshape(b * t, SBW),
                           mix_w_out[i], g[1]).reshape(b, t, d)
        else:
            h1 = _conformer(h, g[0], g[1], conv_w_pw1[i], conv_b_pw1[i], conv_w_dw[i], conv_b_dw[i],
                            conv_ln_g[i], conv_ln_b[i], conv_w_pw2[i], conv_b_pw2[i])
        skip = N_META + PAD_FRONT if layer == depth - 1 else 0
        h = _ffn(h1, g[2], g[3], ffn_w_gate[layer], ffn_w_up[layer], ffn_w_down[layer], skip=skip)
    return h
```

```python
import functools

import jax
import jax.numpy as jnp
from jax import lax
from jax.experimental import pallas as pl
from jax.experimental.pallas import tpu as pltpu

N_META = 16
MLSTM_HEADS = 4
MLSTM_DQK = 128
MLSTM_DV = 256
QK_CONV_WIDTH = 4
GATE_SOFTCAP = 15.0
SB_HEADS = 4
SB_DH = 128
SB_BLOCK = 128
PAD_FRONT = SB_BLOCK - N_META
CONV_WIDTH = 31
MQK = MLSTM_HEADS * MLSTM_DQK
MV = MLSTM_HEADS * MLSTM_DV
SBW = SB_HEADS * SB_DH
NEG = -1e30
EPS = 1e-6

LANES = 128
SUBLANES = 8
V7X_VMEM_BYTES = 64 * 1024 * 1024
F32_EXP_UNDERFLOW = -104.0

MLSTM_CHUNK = 128
MLSTM_BATCH_ROWS = 4
CONV_HALO = 32
CONV_SUBTILES = 2
GATE_LANES = LANES

BF16 = jnp.bfloat16
F32 = jnp.float32


def _vmem_limit(block_bytes, scratch_bytes=0, temp_bytes=0):
    est = 2 * block_bytes + scratch_bytes + temp_bytes + (4 << 20)
    return int(min(max(est, 16 << 20), V7X_VMEM_BYTES - (6 << 20)))


def _nbytes(shape, dtype):
    n = 1
    for s in shape:
        n *= s
    return n * jnp.dtype(dtype).itemsize


def _seq_tile(t, cap=1024):
    best = SB_BLOCK
    for cand in range(SB_BLOCK, cap + 1, SB_BLOCK):
        if t % cand == 0:
            best = cand
    return best


def _rms(x, g):
    return x * lax.rsqrt(jnp.mean(x * x, axis=-1, keepdims=True) + EPS) * g


def _softplus(x):
    return jnp.maximum(x, 0.0) + jnp.log(1.0 + jnp.exp(-jnp.abs(x)))


def _split3(x):
    hi = x.astype(BF16)
    r = x - hi.astype(F32)
    mid = r.astype(BF16)
    lo = (r - mid.astype(F32)).astype(BF16)
    return hi, mid, lo


def _dot(a, b):
    return jnp.dot(a, b, preferred_element_type=F32)


def _dot_nt(a, b):
    return lax.dot_general(a, b, (((1,), (1,)), ((), ())), preferred_element_type=F32)


def _dot_tn(a, b):
    return lax.dot_general(a, b, (((0,), (0,)), ((), ())), preferred_element_type=F32)


def _in_proj_kernel(*refs, embed):
    if embed:
        x_ref, meta_ref, *refs = refs
        *refs, h0_ref, conv_scr = refs
    else:
        h_ref, *refs, conv_scr = refs
    (g_ref, wqk_ref, wv_ref, wo_ref, wg_ref, wsq_ref, wsk_ref, wsv_ref, cw_ref, cb_ref, gb_ref,
     qm_ref, km_ref, vm_ref, om_ref, gt_ref, qs_ref, ks_ref, vs_ref) = refs
    t = pl.program_id(1)
    if embed:
        xt = x_ref[...]
        tm, d = xt.shape
        head = jnp.concatenate([jnp.zeros((PAD_FRONT, d), F32), meta_ref[...],
                                xt[:tm - PAD_FRONT - N_META]], axis=0)
        tile = jnp.where(t == 0, head, xt)
        h0_ref[0] = tile
    else:
        tile = h_ref[0]
        tm = tile.shape[0]
    u = _rms(tile, g_ref[...])
    row = t * tm + lax.broadcasted_iota(jnp.int32, (tm, 1), 0)
    valid = row >= PAD_FRONT
    ub = jnp.where(valid, u, 0.0).astype(BF16)

    @pl.when(t == 0)
    def _():
        conv_scr[0:SUBLANES, :] = jnp.zeros((SUBLANES, 2 * MQK), F32)

    conv_scr[SUBLANES:SUBLANES + tm, :] = _dot(ub, wqk_ref[...])
    acc = cb_ref[...]
    for j in range(QK_CONV_WIDTH):
        shift = SUBLANES - (QK_CONV_WIDTH - 1) + j
        acc = acc + cw_ref[j:j + 1, :] * conv_scr[pl.ds(shift, tm), :]
    conv_scr[0:SUBLANES, :] = conv_scr[tm:tm + SUBLANES, :]
    qk = acc * jax.nn.sigmoid(acc)
    qm_ref[0] = (qk[:, :MQK] * MLSTM_DQK ** -0.5).astype(BF16)
    km_ref[0] = qk[:, MQK:].astype(BF16)

    vm_ref[0] = _dot(ub, wv_ref[...]).astype(BF16)
    om_ref[0] = _dot(ub, wo_ref[...])
    qs_ref[0] = (_dot(ub, wsq_ref[...]) * SB_DH ** -0.5).astype(BF16)
    ks_ref[0] = _dot(ub, wsk_ref[...]).astype(BF16)
    vs_ref[0] = _dot(ub, wsv_ref[...]).astype(BF16)

    gt = _dot(ub, wg_ref[...]) + gb_ref[...]
    gt = GATE_SOFTCAP * jnp.tanh(gt / GATE_SOFTCAP)
    lane = lax.broadcasted_iota(jnp.int32, (1, GATE_LANES), 1)
    log_i = jnp.where(valid, gt, NEG)
    log_f = jnp.where(valid, -_softplus(-gt), 0.0)
    gt_ref[0] = jnp.where(lane < MLSTM_HEADS, log_i, jnp.where(lane < 2 * MLSTM_HEADS, log_f, 0.0))


def _embed_tile(seq):
    tm = _seq_tile(seq + N_META + PAD_FRONT)
    return tm if tm <= seq else None


def _in_proj(h, g0, w_in, conv_w, conv_b, gate_b, meta=None):
    embed = meta is not None
    b, t, d = h.shape
    if embed:
        t += N_META + PAD_FRONT
    tm = _seq_tile(t)
    o = [0, 2 * MQK, 2 * MQK + MV, 2 * MQK + 2 * MV, 2 * MQK + 2 * MV + 2 * MLSTM_HEADS]
    o += [o[-1] + SBW, o[-1] + 2 * SBW, o[-1] + 3 * SBW]
    wb = w_in.astype(BF16)
    wqk, wv, wo = wb[:, o[0]:o[1]], wb[:, o[1]:o[2]], wb[:, o[2]:o[3]]
    wg = jnp.pad(wb[:, o[3]:o[4]], ((0, 0), (0, GATE_LANES - 2 * MLSTM_HEADS)))
    wsq, wsk, wsv = wb[:, o[4]:o[5]], wb[:, o[5]:o[6]], wb[:, o[6]:o[7]]
    gb = jnp.pad(gate_b.astype(F32), (0, GATE_LANES - 2 * MLSTM_HEADS)).reshape(1, GATE_LANES)

    def full(a):
        return pl.BlockSpec(a.shape, lambda i, j: (0,) * a.ndim)

    def rows(width):
        return pl.BlockSpec((1, tm, width), lambda i, j: (i, j, 0))

    params = [g0.reshape(1, d), wqk, wv, wo, wg, wsq, wsk, wsv, conv_w, conv_b.reshape(1, -1), gb]
    out_widths = [(MQK, BF16), (MQK, BF16), (MV, BF16), (MV, F32), (GATE_LANES, F32),
                  (SBW, BF16), (SBW, BF16), (SBW, BF16)]
    if embed:
        offset = N_META + PAD_FRONT
        x_spec = pl.BlockSpec((None, pl.Element(tm), pl.Element(d)),
                              lambda i, j: (i, pl.multiple_of(jnp.maximum(j * tm - offset, 0), SUBLANES), 0))
        ins = [h, meta.astype(F32)] + params
        in_specs = [x_spec, full(ins[1])] + [full(a) for a in params]
        out_widths = out_widths + [(d, F32)]
    else:
        ins = [h] + params
        in_specs = [rows(d)] + [full(a) for a in params]
    blk = _nbytes((tm, d), F32) + sum(_nbytes(a.shape, a.dtype) for a in params)
    blk += sum(_nbytes((tm, w), dt) for w, dt in out_widths)
    scr = _nbytes((tm + 2 * SUBLANES, 2 * MQK), F32)
    return pl.pallas_call(
        functools.partial(_in_proj_kernel, embed=embed),
        grid=(b, t // tm),
        in_specs=in_specs,
        out_specs=[rows(w) for w, _ in out_widths],
        out_shape=[jax.ShapeDtypeStruct((b, t, w), dt) for w, dt in out_widths],
        scratch_shapes=[pltpu.VMEM((tm + 2 * SUBLANES, 2 * MQK), F32)],
        compiler_params=pltpu.CompilerParams(
            dimension_semantics=("arbitrary", "arbitrary"),
            vmem_limit_bytes=_vmem_limit(blk, scr, 4 * _nbytes((tm, 2 * MQK), F32))),
        name="mixer_in_proj",
    )(*ins)


def _mlstm_kernel(q_ref, k_ref, v_ref, o_ref, gt_ref, hg_ref, out_ref, c_scr, n_scr):
    n_heads, dk, dv, ln = MLSTM_HEADS, MLSTM_DQK, MLSTM_DV, MLSTM_CHUNK
    rows = range(q_ref.shape[0])

    @pl.when(pl.program_id(1) == 0)
    def _():
        c_scr[...] = jnp.zeros(c_scr.shape, F32)
        n_scr[...] = jnp.zeros(n_scr.shape, F32)

    ri = lax.broadcasted_iota(jnp.int32, (ln, ln), 0)
    ci = lax.broadcasted_iota(jnp.int32, (ln, ln), 1)
    tril = ci <= ri
    ltri = tril.astype(BF16)
    gates, csum, gates_t, csum_t = [], [], [], []
    for bb in rows:
        g = gt_ref[bb]
        g_hi, g_mid, g_lo = _split3(g)
        cs = _dot(ltri, g_hi) + _dot(ltri, g_mid) + _dot(ltri, g_lo)
        gates.append(g)
        csum.append(cs)
        gates_t.append(g.T)
        csum_t.append(cs.T)

    chains = [(bb, hd) for bb in rows for hd in range(n_heads)]
    ids = range(len(chains))
    q = [q_ref[bb, :, hd * dk:(hd + 1) * dk] for bb, hd in chains]
    k = [k_ref[bb, :, hd * dk:(hd + 1) * dk] for bb, hd in chains]
    v = [v_ref[bb, :, hd * dv:(hd + 1) * dv] for bb, hd in chains]
    c_st = [c_scr[n] for n in ids]
    n_st = [n_scr[n] for n in ids]
    bcol = [csum[bb][:, n_heads + hd:n_heads + hd + 1] for bb, hd in chains]
    ones_k = jnp.ones((ln, LANES), BF16)
    ones_v = jnp.ones((dv, LANES), BF16)
    s_b, q_c, q_n = [], [], []
    for n, (bb, hd) in enumerate(chains):
        brow = csum_t[bb][n_heads + hd:n_heads + hd + 1, :]
        li_row = gates_t[bb][hd:hd + 1, :]
        w_intra = jnp.exp(jnp.where(tril, bcol[n] - brow + li_row, NEG))
        s_b.append((_dot_nt(q[n], k[n]) * w_intra).astype(BF16))
        q_c.append(_dot(q[n], c_st[n].astype(BF16)))
        q_n.append(_dot(q[n], n_st[n].astype(BF16)))
    s_v = [_dot(s_b[n], v[n]) for n in ids]
    s_1 = [_dot(s_b[n], ones_k) for n in ids]

    for n, (bb, hd) in enumerate(chains):
        w_inter = jnp.broadcast_to(jnp.exp(bcol[n]), (ln, LANES))
        num = s_v[n] + jnp.concatenate([w_inter] * (dv // LANES), axis=1) * q_c[n]
        den = jnp.maximum(jnp.abs(s_1[n] + w_inter * q_n[n]), 1.0)
        msq = _dot((num * num).astype(BF16), ones_v) * (1.0 / dv)
        scale = lax.rsqrt(msq + EPS * den * den)
        hn = num * jnp.concatenate([scale] * (dv // LANES), axis=1) * hg_ref[:, hd * dv:(hd + 1) * dv]
        gate = jax.nn.sigmoid(o_ref[bb, :, hd * dv:(hd + 1) * dv])
        out_ref[bb, :, hd * dv:(hd + 1) * dv] = (hn * gate).astype(BF16)

    for n, (bb, hd) in enumerate(chains):
        g_tot = csum[bb][ln - 1:ln, n_heads + hd:n_heads + hd + 1]
        wa = jnp.exp(g_tot - bcol[n] + gates[bb][:, hd:hd + 1])
        wc = jnp.exp(g_tot)
        kw = (k[n].astype(F32) * wa).astype(BF16)
        c_scr[n] = wc * c_st[n] + _dot_tn(kw, v[n])
        n_scr[n] = wc * n_st[n] + _dot_tn(kw, ones_k)


def _mlstm(qm, km, vm, om, gt, hnorm_g):
    b, t, _ = qm.shape
    ln = MLSTM_CHUNK
    nb = _batch_rows(b, most=MLSTM_BATCH_ROWS)

    def rows(width):
        return pl.BlockSpec((nb, ln, width), lambda i, j: (i, j, 0))

    blk = nb * (2 * _nbytes((ln, MQK), BF16) + 2 * _nbytes((ln, MV), BF16) + _nbytes((ln, MV), F32)
                + _nbytes((ln, GATE_LANES), F32)) + _nbytes((1, MV), F32)
    scr = (_nbytes((nb * MLSTM_HEADS, MLSTM_DQK, MLSTM_DV), F32)
           + _nbytes((nb * MLSTM_HEADS, MLSTM_DQK, LANES), F32))
    return pl.pallas_call(
        _mlstm_kernel,
        grid=(b // nb, t // ln),
        in_specs=[rows(MQK), rows(MQK), rows(MV), rows(MV), rows(GATE_LANES),
                  pl.BlockSpec((1, MV), lambda i, j: (0, 0))],
        out_specs=rows(MV),
        out_shape=jax.ShapeDtypeStruct((b, t, MV), BF16),
        scratch_shapes=[pltpu.VMEM((nb * MLSTM_HEADS, MLSTM_DQK, MLSTM_DV), F32),
                        pltpu.VMEM((nb * MLSTM_HEADS, MLSTM_DQK, LANES), F32)],
        compiler_params=pltpu.CompilerParams(
            dimension_semantics=("arbitrary", "arbitrary"),
            vmem_limit_bytes=_vmem_limit(blk, scr, 16 << 20)),
        name="mlstm_chunkwise",
    )(qm, km, vm, om, gt, hnorm_g.reshape(1, MV))


def _sb_block(q_ref, k_ref, v_ref, cmat_ref, across_scr, acc_scr, i, j, *, causal, validity, first):
    blk = SB_BLOCK
    start = pl.multiple_of(j * blk, blk)
    mask = None
    if causal or validity:
        ri = lax.broadcasted_iota(jnp.int32, (blk, blk), 0)
        s_idx = j * blk + lax.broadcasted_iota(jnp.int32, (blk, blk), 1)
        if causal:
            mask = s_idx < i * blk + ri
        if validity:
            ok = s_idx >= PAD_FRONT
            mask = ok if mask is None else jnp.logical_and(mask, ok)
    chains = [(bb, slice(hd * SB_DH, (hd + 1) * SB_DH)) for bb in range(q_ref.shape[0]) for hd in range(SB_HEADS)]
    ids = range(len(chains))
    z = [_dot_nt(q_ref[bb, :, c], k_ref[bb, pl.ds(start, blk), c]) for bb, c in chains]
    both = []
    for n in ids:
        log1m = -_softplus(z[n])
        if mask is not None:
            log1m = jnp.where(mask, log1m, 0.0)
        hi = log1m.astype(BF16)
        lo = (log1m - hi.astype(F32)).astype(BF16)
        both.append(_dot(jnp.concatenate([hi, lo], axis=1), cmat_ref[...]))
    prev = [None if first else across_scr[n] for n in ids]
    pv = []
    for n, (bb, c) in enumerate(chains):
        log_w = z[n] + both[n][:, :blk]
        if not first:
            log_w = log_w + prev[n]
        if mask is not None:
            log_w = jnp.where(mask, log_w, NEG)
        pv.append(_dot(jnp.exp(log_w).astype(BF16), v_ref[bb, pl.ds(start, blk), c]))
    amax = None
    for n, (bb, c) in enumerate(chains):
        across = both[n][:, blk:]
        if first:
            acc_scr[bb, :, c] = pv[n]
        else:
            acc_scr[bb, :, c] += pv[n]
            across = across + prev[n]
        across_scr[n] = across
        amax = across if amax is None else jnp.maximum(amax, across)
    return (jnp.max(amax) > F32_EXP_UNDERFLOW).astype(jnp.int32)


def _sb_kernel(q_ref, k_ref, v_ref, out_ref, cmat_ref, across_scr, acc_scr):
    blk = SB_BLOCK
    i = pl.program_id(1)

    @pl.when(jnp.logical_and(pl.program_id(0) == 0, i == 0))
    def _():
        ri = lax.broadcasted_iota(jnp.int32, (2 * blk, 2 * blk), 0)
        ci = lax.broadcasted_iota(jnp.int32, (2 * blk, 2 * blk), 1)
        cmat_ref[...] = jnp.logical_or(ci >= blk, (ri % blk) >= ci).astype(BF16)

    block = functools.partial(_sb_block, q_ref, k_ref, v_ref, cmat_ref, across_scr, acc_scr, i)
    go = block(i, causal=True, validity=True, first=True)

    def cond(carry):
        j, go = carry
        return jnp.logical_and(j >= 1, go > 0)

    def body(carry):
        j, _ = carry
        return j - 1, block(j, causal=False, validity=False, first=False)

    j, go = lax.while_loop(cond, body, (i - 1, go))

    @pl.when(jnp.logical_and(j == 0, go > 0))
    def _():
        block(0, causal=False, validity=True, first=False)

    out_ref[...] = acc_scr[...].astype(BF16)


def _batch_rows(b, most=2):
    rows = most
    while b % rows:
        rows //= 2
    return rows


def _stick_breaking(qs, ks, vs):
    b, t, _ = qs.shape
    blk = SB_BLOCK
    nb = _batch_rows(b)
    scr = (_nbytes((2 * blk, 2 * blk), BF16) + _nbytes((nb * SB_HEADS, blk, blk), F32)
           + _nbytes((nb, blk, SBW), F32))
    vmem = 4 * _nbytes((nb, blk, SBW), BF16) + 2 * _nbytes((nb, t, SBW), BF16) + scr + (8 << 20)
    resident = dict(pipeline_mode=pl.Buffered(1))
    return pl.pallas_call(
        _sb_kernel,
        grid=(b // nb, t // blk),
        in_specs=[pl.BlockSpec((nb, blk, SBW), lambda bi, qi: (bi, qi, 0)),
                  pl.BlockSpec((nb, t, SBW), lambda bi, qi: (bi, 0, 0), **resident),
                  pl.BlockSpec((nb, t, SBW), lambda bi, qi: (bi, 0, 0), **resident)],
        out_specs=pl.BlockSpec((nb, blk, SBW), lambda bi, qi: (bi, qi, 0)),
        out_shape=jax.ShapeDtypeStruct((b, t, SBW), BF16),
        scratch_shapes=[pltpu.VMEM((2 * blk, 2 * blk), BF16),
                        pltpu.VMEM((nb * SB_HEADS, blk, blk), F32),
                        pltpu.VMEM((nb, blk, SBW), F32)],
        compiler_params=pltpu.CompilerParams(
            dimension_semantics=("arbitrary", "arbitrary"),
            vmem_limit_bytes=int(vmem)),
        name="stick_breaking",
    )(qs, ks, vs)


def _out_proj_kernel(h_ref, hm_ref, hs_ref, wm_ref, ws_ref, g_ref, out_ref):
    y = _dot(hm_ref[...], wm_ref[...]) + _dot(hs_ref[...], ws_ref[...])
    out_ref[...] = h_ref[...] + _rms(y, g_ref[...])


def _out_proj(h2, hm2, hs2, w_out, g1):
    n, d = h2.shape
    tm = _seq_tile(n)
    wb = w_out.astype(BF16)
    wm, ws = wb[:MV], wb[MV:]
    blk = (2 * _nbytes((tm, d), F32) + _nbytes((tm, MV), BF16) + _nbytes((tm, SBW), BF16)
           + _nbytes(wb.shape, BF16) + _nbytes((1, d), F32))
    return pl.pallas_call(
        _out_proj_kernel,
        grid=(n // tm,),
        in_specs=[pl.BlockSpec((tm, d), lambda i: (i, 0)),
                  pl.BlockSpec((tm, MV), lambda i: (i, 0)),
                  pl.BlockSpec((tm, SBW), lambda i: (i, 0)),
                  pl.BlockSpec(wm.shape, lambda i: (0, 0)),
                  pl.BlockSpec(ws.shape, lambda i: (0, 0)),
                  pl.BlockSpec((1, d), lambda i: (0, 0))],
        out_specs=pl.BlockSpec((tm, d), lambda i: (i, 0)),
        out_shape=jax.ShapeDtypeStruct((n, d), F32),
        compiler_params=pltpu.CompilerParams(
            dimension_semantics=("arbitrary",),
            vmem_limit_bytes=_vmem_limit(blk, 0, 2 * _nbytes((tm, d), F32))),
        name="mixer_out_proj",
    )(h2, hm2, hs2, wm, ws, g1.reshape(1, d))


def _conformer_kernel(h_ref, g0_ref, w1_ref, b1_ref, wdw_ref, bdw_ref, lng_ref, lnb_ref,
                      w2_ref, b2_ref, g1_ref, out_ref, conv_scr):
    t = pl.program_id(1)
    tm, d = h_ref.shape[1], h_ref.shape[2]
    ts = tm // CONV_SUBTILES
    subs = range(CONV_SUBTILES)

    @pl.when(t == 0)
    def _():
        conv_scr[0:CONV_HALO, :] = jnp.zeros((CONV_HALO, d), F32)

    for s in subs:
        ub = _rms(h_ref[0, s * ts:(s + 1) * ts], g0_ref[...]).astype(BF16)
        ag = _dot(ub, w1_ref[...]) + b1_ref[...]
        row = t * tm + s * ts + lax.broadcasted_iota(jnp.int32, (ts, 1), 0)
        y = jnp.where(row >= PAD_FRONT, ag[:, :d] * jax.nn.sigmoid(ag[:, d:]), 0.0)
        conv_scr[CONV_HALO + s * ts:CONV_HALO + (s + 1) * ts, :] = y

    conv = []
    for s in subs:
        acc = None
        for r in range(SUBLANES):
            part = None
            for a in range(CONV_HALO // SUBLANES):
                lag = SUBLANES * a + r
                if lag >= CONV_WIDTH:
                    continue
                j = CONV_WIDTH - 1 - lag
                start = CONV_HALO + s * ts - SUBLANES * (a + 1)
                term = wdw_ref[j:j + 1, :] * conv_scr[start:start + ts + SUBLANES, :]
                part = term if part is None else part + term
            if r:
                part = pltpu.roll(part, r, axis=0)
            acc = part if acc is None else acc + part
        conv.append(acc[SUBLANES:SUBLANES + ts] + bdw_ref[...])

    for s in subs:
        mu = jnp.mean(conv[s], axis=-1, keepdims=True)
        cen = conv[s] - mu
        var = jnp.mean(cen * cen, axis=-1, keepdims=True)
        ln = cen * lax.rsqrt(var + EPS) * lng_ref[...] + lnb_ref[...]
        act = (ln * jax.nn.sigmoid(ln)).astype(BF16)
        z = _dot(act, w2_ref[...]) + b2_ref[...]
        out_ref[0, s * ts:(s + 1) * ts] = h_ref[0, s * ts:(s + 1) * ts] + _rms(z, g1_ref[...])

    conv_scr[0:CONV_HALO, :] = conv_scr[tm:tm + CONV_HALO, :]


def _conformer(h, g0, g1, w_pw1, b_pw1, w_dw, b_dw, ln_g, ln_b, w_pw2, b_pw2):
    b, t, d = h.shape
    tm = _seq_tile(t)
    ins = [h, g0.reshape(1, d), w_pw1.astype(BF16), b_pw1.reshape(1, -1), w_dw, b_dw.reshape(1, d),
           ln_g.reshape(1, d), ln_b.reshape(1, d), w_pw2.astype(BF16), b_pw2.reshape(1, d), g1.reshape(1, d)]

    def full(a):
        return pl.BlockSpec(a.shape, lambda i, j: (0,) * a.ndim)

    blk = 2 * _nbytes((tm, d), F32) + sum(_nbytes(a.shape, a.dtype) for a in ins[1:])
    assert tm % (CONV_SUBTILES * CONV_HALO) == 0
    scr = _nbytes((tm + CONV_HALO, d), F32)
    return pl.pallas_call(
        _conformer_kernel,
        grid=(b, t // tm),
        in_specs=[pl.BlockSpec((1, tm, d), lambda i, j: (i, j, 0))] + [full(a) for a in ins[1:]],
        out_specs=pl.BlockSpec((1, tm, d), lambda i, j: (i, j, 0)),
        out_shape=jax.ShapeDtypeStruct((b, t, d), F32),
        scratch_shapes=[pltpu.VMEM((tm + CONV_HALO, d), F32)],
        compiler_params=pltpu.CompilerParams(
            dimension_semantics=("arbitrary", "arbitrary"),
            vmem_limit_bytes=_vmem_limit(blk, scr, 6 * _nbytes((tm, d), F32))),
        name="conformer_conv",
    )(*ins)


def _ffn_kernel(h_ref, res_ref, g2_ref, wg_ref, wu_ref, wd_ref, g3_ref, out_ref,
                u_even, u_odd, acc_even, acc_odd, *, chunks):
    i = pl.program_id(0)

    @pl.when(i == 0)
    def _():
        for ref in (u_even, u_odd, acc_even, acc_odd):
            ref[...] = jnp.zeros(ref.shape, ref.dtype)

    def step(u_new, u_prev, acc_new, acc_prev):
        out_ref[...] = res_ref[...] + _rms(acc_prev[...], g3_ref[...])
        u_new[...] = _rms(h_ref[...], g2_ref[...]).astype(BF16)
        u = u_prev[...]
        acc = None
        for lo, hi in chunks:
            a = _dot(u, wg_ref[:, lo:hi])
            hid = (a * jax.nn.sigmoid(a) * _dot(u, wu_ref[:, lo:hi])).astype(BF16)
            part = _dot(hid, wd_ref[lo:hi, :])
            acc = part if acc is None else acc + part
        acc_new[...] = acc

    @pl.when(i % 2 == 0)
    def _():
        step(u_even, u_odd, acc_odd, acc_even)

    @pl.when(i % 2 == 1)
    def _():
        step(u_odd, u_even, acc_even, acc_odd)


FFN_PIPELINE_LAG = 2
MXU_DIM = 256


def _ffn_chunks(hidden):
    if hidden % MXU_DIM:
        return ((0, hidden),)
    units = hidden // MXU_DIM
    pieces = -(-units // 6)
    bounds = [round(units * p / pieces) * MXU_DIM for p in range(pieces + 1)]
    return tuple(zip(bounds[:-1], bounds[1:]))


def _ffn(h, g2, g3, w_gate, w_up, w_down, skip=0):
    b, t, d = h.shape
    hidden = w_gate.shape[1]
    chunks = _ffn_chunks(hidden)
    weights = (g2.reshape(1, d), w_gate.astype(BF16), w_up.astype(BF16), w_down.astype(BF16), g3.reshape(1, d))
    lag = FFN_PIPELINE_LAG
    if skip == 0:
        rows = b * t
        tm = _seq_tile(rows, cap=512)
        n_tiles = rows // tm
        operand = h.reshape(rows, d)
        in_tile = lambda i: (jnp.minimum(i, n_tiles - 1), 0)
        out_tile = lambda i: (jnp.maximum(i - lag, 0), 0)
        h_spec = pl.BlockSpec((tm, d), in_tile)
        res_spec = pl.BlockSpec((tm, d), out_tile)
        out_spec = pl.BlockSpec((tm, d), out_tile)
        out_shape = jax.ShapeDtypeStruct((rows, d), F32)
    else:
        tm = _seq_tile(t - skip, cap=512)
        per_row = (t - skip) // tm
        n_tiles = b * per_row
        operand = h
        assert skip % SUBLANES == 0

        def rows_at(tile):
            return (tile // per_row, pl.multiple_of(skip + (tile % per_row) * tm, SUBLANES), 0)

        def out_tile(i):
            tile = jnp.maximum(i - lag, 0)
            return (tile // per_row, tile % per_row, 0)

        elem = (None, pl.Element(tm), pl.Element(d))
        h_spec = pl.BlockSpec(elem, lambda i: rows_at(jnp.minimum(i, n_tiles - 1)))
        res_spec = pl.BlockSpec(elem, lambda i: rows_at(jnp.maximum(i - lag, 0)))
        out_spec = pl.BlockSpec((None, tm, d), out_tile)
        out_shape = jax.ShapeDtypeStruct((b, t - skip, d), F32)

    def resident(a):
        return pl.BlockSpec(a.shape, lambda i: (0,) * a.ndim, pipeline_mode=pl.Buffered(1))

    widest = max(hi - lo for lo, hi in chunks)
    vmem = (6 * _nbytes((tm, d), F32) + sum(_nbytes(a.shape, a.dtype) for a in weights)
            + 2 * _nbytes((tm, d), BF16) + 2 * _nbytes((tm, d), F32)
            + len(chunks) * 3 * _nbytes((tm, widest), F32) + 2 * _nbytes((tm, d), F32) + (4 << 20))
    out = pl.pallas_call(
        functools.partial(_ffn_kernel, chunks=chunks),
        grid=(n_tiles + lag,),
        in_specs=[h_spec, res_spec] + [resident(a) for a in weights],
        out_specs=out_spec,
        out_shape=out_shape,
        scratch_shapes=[pltpu.VMEM((tm, d), BF16), pltpu.VMEM((tm, d), BF16),
                        pltpu.VMEM((tm, d), F32), pltpu.VMEM((tm, d), F32)],
        compiler_params=pltpu.CompilerParams(
            dimension_semantics=("arbitrary",),
            vmem_limit_bytes=int(min(vmem, V7X_VMEM_BYTES - (6 << 20)))),
        name="swiglu_ffn",
    )(operand, operand, *weights)
    return out if skip else out.reshape(b, t, d)


def kernel(x, meta, norm_g, mix_w_in, mix_qk_conv_w, mix_qk_conv_b, mix_gate_b, mix_hnorm_g, mix_w_out,
           conv_w_pw1, conv_b_pw1, conv_w_dw, conv_b_dw, conv_ln_g, conv_ln_b, conv_w_pw2, conv_b_pw2,
           ffn_w_gate, ffn_w_up, ffn_w_down):
    b, seq, d = x.shape
    depth = norm_g.shape[0]
    t = seq + N_META + PAD_FRONT
    h = None
    if _embed_tile(seq) is None:
        h = jnp.concatenate([jnp.zeros((b, PAD_FRONT, d), x.dtype),
                             jnp.broadcast_to(meta[None].astype(x.dtype), (b, N_META, d)), x], axis=1)
    for layer in range(depth):
        g = norm_g[layer]
        i = layer // 2
        if layer % 2 == 0:
            if h is None:
                qm, km, vm, om, gt, qs, ks, vs, h = _in_proj(x, g[0], mix_w_in[i], mix_qk_conv_w[i],
                                                             mix_qk_conv_b[i], mix_gate_b[i], meta=meta)
            else:
                qm, km, vm, om, gt, qs, ks, vs = _in_proj(h, g[0], mix_w_in[i], mix_qk_conv_w[i],
                                                          mix_qk_conv_b[i], mix_gate_b[i])
            hm = _mlstm(qm, km, vm, om, gt, mix_hnorm_g[i])
            hs = _stick_breaking(qs, ks, vs)
            h1 = _out_proj(h.reshape(b * t, d), hm.reshape(b * t, MV), hs.reshape(b * t, SBW),
                           mix_w_out[i], g[1]).reshape(b, t, d)
        else:
            h1 = _conformer(h, g[0], g[1], conv_w_pw1[i], conv_b_pw1[i], conv_w_dw[i], conv_b_dw[i],
                            conv_ln_g[i], conv_ln_b[i], conv_w_pw2[i], conv_b_pw2[i])
        skip = N_META + PAD_FRONT if layer == depth - 1 else 0
        h = _ffn(h1, g[2], g[3], ffn_w_gate[layer], ffn_w_up[layer], ffn_w_down[layer], skip=skip)
    return h
```

```python
import functools

import jax
import jax.numpy as jnp
from jax import lax
from jax.experimental import pallas as pl
from jax.experimental.pallas import tpu as pltpu

N_META = 16
MLSTM_HEADS = 4
MLSTM_DQK = 128
MLSTM_DV = 256
QK_CONV_WIDTH = 4
GATE_SOFTCAP = 15.0
SB_HEADS = 4
SB_DH = 128
SB_BLOCK = 128
PAD_FRONT = SB_BLOCK - N_META
CONV_WIDTH = 31
MQK = MLSTM_HEADS * MLSTM_DQK
MV = MLSTM_HEADS * MLSTM_DV
SBW = SB_HEADS * SB_DH
NEG = -1e30
EPS = 1e-6

LANES = 128
SUBLANES = 8
V7X_VMEM_BYTES = 64 * 1024 * 1024
BF16_EXP_UNDERFLOW = -93.0

MLSTM_CHUNK = 128
MLSTM_BATCH_ROWS = 4
CONV_HALO = 32
CONV_SUBTILES = 2
GATE_LANES = LANES

BF16 = jnp.bfloat16
F32 = jnp.float32


def _vmem_limit(block_bytes, scratch_bytes=0, temp_bytes=0):
    est = 2 * block_bytes + scratch_bytes + temp_bytes + (4 << 20)
    return int(min(max(est, 16 << 20), V7X_VMEM_BYTES - (6 << 20)))


def _nbytes(shape, dtype):
    n = 1
    for s in shape:
        n *= s
    return n * jnp.dtype(dtype).itemsize


def _seq_tile(t, cap=1024):
    best = SB_BLOCK
    for cand in range(SB_BLOCK, cap + 1, SB_BLOCK):
        if t % cand == 0:
            best = cand
    return best


def _rms(x, g):
    return x * lax.rsqrt(jnp.mean(x * x, axis=-1, keepdims=True) + EPS) * g


def _softplus(x):
    return jnp.maximum(x, 0.0) + jnp.log(1.0 + jnp.exp(-jnp.abs(x)))


def _split3(x):
    hi = x.astype(BF16)
    r = x - hi.astype(F32)
    mid = r.astype(BF16)
    lo = (r - mid.astype(F32)).astype(BF16)
    return hi, mid, lo


def _dot(a, b):
    return jnp.dot(a, b, preferred_element_type=F32)


def _dot_nt(a, b):
    return lax.dot_general(a, b, (((1,), (1,)), ((), ())), preferred_element_type=F32)


def _dot_tn(a, b):
    return lax.dot_general(a, b, (((0,), (0,)), ((), ())), preferred_element_type=F32)


def _in_proj_kernel(*refs, embed):
    if embed:
        x_ref, meta_ref, *refs = refs
        *refs, h0_ref, conv_scr = refs
    else:
        h_ref, *refs, conv_scr = refs
    (g_ref, wqk_ref, wv_ref, wo_ref, wg_ref, wsq_ref, wsk_ref, wsv_ref, cw_ref, cb_ref, gb_ref,
     qm_ref, km_ref, vm_ref, om_ref, gt_ref, qs_ref, ks_ref, vs_ref) = refs
    t = pl.program_id(1)
    if embed:
        xt = x_ref[...]
        tm, d = xt.shape
        head = jnp.concatenate([jnp.zeros((PAD_FRONT, d), F32), meta_ref[...],
                                xt[:tm - PAD_FRONT - N_META]], axis=0)
        tile = jnp.where(t == 0, head, xt)
        h0_ref[0] = tile
    else:
        tile = h_ref[0]
        tm = tile.shape[0]
    u = _rms(tile, g_ref[...])
    row = t * tm + lax.broadcasted_iota(jnp.int32, (tm, 1), 0)
    valid = row >= PAD_FRONT
    ub = jnp.where(valid, u, 0.0).astype(BF16)

    @pl.when(t == 0)
    def _():
        conv_scr[0:SUBLANES, :] = jnp.zeros((SUBLANES, 2 * MQK), F32)

    conv_scr[SUBLANES:SUBLANES + tm, :] = _dot(ub, wqk_ref[...])
    acc = cb_ref[...]
    for j in range(QK_CONV_WIDTH):
        shift = SUBLANES - (QK_CONV_WIDTH - 1) + j
        acc = acc + cw_ref[j:j + 1, :] * conv_scr[pl.ds(shift, tm), :]
    conv_scr[0:SUBLANES, :] = conv_scr[tm:tm + SUBLANES, :]
    qk = acc * jax.nn.sigmoid(acc)
    qm_ref[0] = (qk[:, :MQK] * MLSTM_DQK ** -0.5).astype(BF16)
    km_ref[0] = qk[:, MQK:].astype(BF16)

    vm_ref[0] = _dot(ub, wv_ref[...]).astype(BF16)
    om_ref[0] = _dot(ub, wo_ref[...])
    qs_ref[0] = (_dot(ub, wsq_ref[...]) * SB_DH ** -0.5).astype(BF16)
    ks_ref[0] = _dot(ub, wsk_ref[...]).astype(BF16)
    vs_ref[0] = _dot(ub, wsv_ref[...]).astype(BF16)

    gt = _dot(ub, wg_ref[...]) + gb_ref[...]
    gt = GATE_SOFTCAP * jnp.tanh(gt / GATE_SOFTCAP)
    lane = lax.broadcasted_iota(jnp.int32, (1, GATE_LANES), 1)
    log_i = jnp.where(valid, gt, NEG)
    log_f = jnp.where(valid, -_softplus(-gt), 0.0)
    gt_ref[0] = jnp.where(lane < MLSTM_HEADS, log_i, jnp.where(lane < 2 * MLSTM_HEADS, log_f, 0.0))


def _embed_tile(seq):
    tm = _seq_tile(seq + N_META + PAD_FRONT)
    return tm if tm <= seq else None


def _in_proj(h, g0, w_in, conv_w, conv_b, gate_b, meta=None):
    embed = meta is not None
    b, t, d = h.shape
    if embed:
        t += N_META + PAD_FRONT
    tm = _seq_tile(t)
    o = [0, 2 * MQK, 2 * MQK + MV, 2 * MQK + 2 * MV, 2 * MQK + 2 * MV + 2 * MLSTM_HEADS]
    o += [o[-1] + SBW, o[-1] + 2 * SBW, o[-1] + 3 * SBW]
    wb = w_in.astype(BF16)
    wqk, wv, wo = wb[:, o[0]:o[1]], wb[:, o[1]:o[2]], wb[:, o[2]:o[3]]
    wg = jnp.pad(wb[:, o[3]:o[4]], ((0, 0), (0, GATE_LANES - 2 * MLSTM_HEADS)))
    wsq, wsk, wsv = wb[:, o[4]:o[5]], wb[:, o[5]:o[6]], wb[:, o[6]:o[7]]
    gb = jnp.pad(gate_b.astype(F32), (0, GATE_LANES - 2 * MLSTM_HEADS)).reshape(1, GATE_LANES)

    def full(a):
        return pl.BlockSpec(a.shape, lambda i, j: (0,) * a.ndim)

    def rows(width):
        return pl.BlockSpec((1, tm, width), lambda i, j: (i, j, 0))

    params = [g0.reshape(1, d), wqk, wv, wo, wg, wsq, wsk, wsv, conv_w, conv_b.reshape(1, -1), gb]
    out_widths = [(MQK, BF16), (MQK, BF16), (MV, BF16), (MV, F32), (GATE_LANES, F32),
                  (SBW, BF16), (SBW, BF16), (SBW, BF16)]
    if embed:
        offset = N_META + PAD_FRONT
        x_spec = pl.BlockSpec((None, pl.Element(tm), pl.Element(d)),
                              lambda i, j: (i, pl.multiple_of(jnp.maximum(j * tm - offset, 0), SUBLANES), 0))
        ins = [h, meta.astype(F32)] + params
        in_specs = [x_spec, full(ins[1])] + [full(a) for a in params]
        out_widths = out_widths + [(d, F32)]
    else:
        ins = [h] + params
        in_specs = [rows(d)] + [full(a) for a in params]
    blk = _nbytes((tm, d), F32) + sum(_nbytes(a.shape, a.dtype) for a in params)
    blk += sum(_nbytes((tm, w), dt) for w, dt in out_widths)
    scr = _nbytes((tm + 2 * SUBLANES, 2 * MQK), F32)
    return pl.pallas_call(
        functools.partial(_in_proj_kernel, embed=embed),
        grid=(b, t // tm),
        in_specs=in_specs,
        out_specs=[rows(w) for w, _ in out_widths],
        out_shape=[jax.ShapeDtypeStruct((b, t, w), dt) for w, dt in out_widths],
        scratch_shapes=[pltpu.VMEM((tm + 2 * SUBLANES, 2 * MQK), F32)],
        compiler_params=pltpu.CompilerParams(
            dimension_semantics=("arbitrary", "arbitrary"),
            vmem_limit_bytes=_vmem_limit(blk, scr, 4 * _nbytes((tm, 2 * MQK), F32))),
        name="mixer_in_proj",
    )(*ins)


def _mlstm_kernel(q_ref, k_ref, v_ref, o_ref, gt_ref, hg_ref, out_ref, c_scr, n_scr):
    n_heads, dk, dv, ln = MLSTM_HEADS, MLSTM_DQK, MLSTM_DV, MLSTM_CHUNK
    rows = range(q_ref.shape[0])

    @pl.when(pl.program_id(1) == 0)
    def _():
        c_scr[...] = jnp.zeros(c_scr.shape, F32)
        n_scr[...] = jnp.zeros(n_scr.shape, F32)

    ri = lax.broadcasted_iota(jnp.int32, (ln, ln), 0)
    ci = lax.broadcasted_iota(jnp.int32, (ln, ln), 1)
    tril = ci <= ri
    ltri = tril.astype(BF16)
    gates, csum, gates_t, csum_t = [], [], [], []
    for bb in rows:
        g = gt_ref[bb]
        g_hi, g_mid, g_lo = _split3(g)
        cs = _dot(ltri, g_hi) + _dot(ltri, g_mid) + _dot(ltri, g_lo)
        gates.append(g)
        csum.append(cs)
        gates_t.append(g.T)
        csum_t.append(cs.T)

    chains = [(bb, hd) for bb in rows for hd in range(n_heads)]
    ids = range(len(chains))
    q = [q_ref[bb, :, hd * dk:(hd + 1) * dk] for bb, hd in chains]
    k = [k_ref[bb, :, hd * dk:(hd + 1) * dk] for bb, hd in chains]
    v = [v_ref[bb, :, hd * dv:(hd + 1) * dv] for bb, hd in chains]
    c_st = [c_scr[n] for n in ids]
    n_st = [n_scr[n] for n in ids]
    bcol = [csum[bb][:, n_heads + hd:n_heads + hd + 1] for bb, hd in chains]
    ones_k = jnp.ones((ln, LANES), BF16)
    ones_v = jnp.ones((dv, LANES), BF16)
    s_b, q_c, q_n = [], [], []
    for n, (bb, hd) in enumerate(chains):
        brow = csum_t[bb][n_heads + hd:n_heads + hd + 1, :]
        li_row = gates_t[bb][hd:hd + 1, :]
        w_intra = jnp.exp(jnp.where(tril, bcol[n] - brow + li_row, NEG))
        s_b.append((_dot_nt(q[n], k[n]) * w_intra).astype(BF16))
        q_c.append(_dot(q[n], c_st[n].astype(BF16)))
        q_n.append(_dot(q[n], n_st[n].astype(BF16)))
    s_v = [_dot(s_b[n], v[n]) for n in ids]
    s_1 = [_dot(s_b[n], ones_k) for n in ids]

    for n, (bb, hd) in enumerate(chains):
        w_inter = jnp.broadcast_to(jnp.exp(bcol[n]), (ln, LANES))
        num = s_v[n] + jnp.concatenate([w_inter] * (dv // LANES), axis=1) * q_c[n]
        den = jnp.maximum(jnp.abs(s_1[n] + w_inter * q_n[n]), 1.0)
        msq = _dot((num * num).astype(BF16), ones_v) * (1.0 / dv)
        scale = lax.rsqrt(msq + EPS * den * den)
        hn = num * jnp.concatenate([scale] * (dv // LANES), axis=1) * hg_ref[:, hd * dv:(hd + 1) * dv]
        gate = jax.nn.sigmoid(o_ref[bb, :, hd * dv:(hd + 1) * dv])
        out_ref[bb, :, hd * dv:(hd + 1) * dv] = (hn * gate).astype(BF16)

    for n, (bb, hd) in enumerate(chains):
        g_tot = csum[bb][ln - 1:ln, n_heads + hd:n_heads + hd + 1]
        wa = jnp.exp(g_tot - bcol[n] + gates[bb][:, hd:hd + 1])
        wc = jnp.exp(g_tot)
        kw = (k[n].astype(F32) * wa).astype(BF16)
        c_scr[n] = wc * c_st[n] + _dot_tn(kw, v[n])
        n_scr[n] = wc * n_st[n] + _dot_tn(kw, ones_k)


def _mlstm(qm, km, vm, om, gt, hnorm_g):
    b, t, _ = qm.shape
    ln = MLSTM_CHUNK
    nb = _batch_rows(b, most=MLSTM_BATCH_ROWS)

    def rows(width):
        return pl.BlockSpec((nb, ln, width), lambda i, j: (i, j, 0))

    blk = nb * (2 * _nbytes((ln, MQK), BF16) + 2 * _nbytes((ln, MV), BF16) + _nbytes((ln, MV), F32)
                + _nbytes((ln, GATE_LANES), F32)) + _nbytes((1, MV), F32)
    scr = (_nbytes((nb * MLSTM_HEADS, MLSTM_DQK, MLSTM_DV), F32)
           + _nbytes((nb * MLSTM_HEADS, MLSTM_DQK, LANES), F32))
    return pl.pallas_call(
        _mlstm_kernel,
        grid=(b // nb, t // ln),
        in_specs=[rows(MQK), rows(MQK), rows(MV), rows(MV), rows(GATE_LANES),
                  pl.BlockSpec((1, MV), lambda i, j: (0, 0))],
        out_specs=rows(MV),
        out_shape=jax.ShapeDtypeStruct((b, t, MV), BF16),
        scratch_shapes=[pltpu.VMEM((nb * MLSTM_HEADS, MLSTM_DQK, MLSTM_DV), F32),
                        pltpu.VMEM((nb * MLSTM_HEADS, MLSTM_DQK, LANES), F32)],
        compiler_params=pltpu.CompilerParams(
            dimension_semantics=("arbitrary", "arbitrary"),
            vmem_limit_bytes=_vmem_limit(blk, scr, 16 << 20)),
        name="mlstm_chunkwise",
    )(qm, km, vm, om, gt, hnorm_g.reshape(1, MV))


def _sb_block(q_ref, k_ref, v_ref, cmat_ref, across_scr, acc_scr, i, j, *, causal, validity, first):
    blk = SB_BLOCK
    start = pl.multiple_of(j * blk, blk)
    mask = None
    if causal or validity:
        ri = lax.broadcasted_iota(jnp.int32, (blk, blk), 0)
        s_idx = j * blk + lax.broadcasted_iota(jnp.int32, (blk, blk), 1)
        if causal:
            mask = s_idx < i * blk + ri
        if validity:
            ok = s_idx >= PAD_FRONT
            mask = ok if mask is None else jnp.logical_and(mask, ok)
    chains = [(bb, slice(hd * SB_DH, (hd + 1) * SB_DH)) for bb in range(q_ref.shape[0]) for hd in range(SB_HEADS)]
    ids = range(len(chains))
    z = [_dot_nt(q_ref[bb, :, c], k_ref[bb, pl.ds(start, blk), c]) for bb, c in chains]
    both = []
    for n in ids:
        log1m = -_softplus(z[n])
        if mask is not None:
            log1m = jnp.where(mask, log1m, 0.0)
        hi = log1m.astype(BF16)
        lo = (log1m - hi.astype(F32)).astype(BF16)
        both.append(_dot(jnp.concatenate([hi, lo], axis=1), cmat_ref[...]))
    prev = [None if first else across_scr[n] for n in ids]
    pv = []
    for n, (bb, c) in enumerate(chains):
        log_w = z[n] + both[n][:, :blk]
        if not first:
            log_w = log_w + prev[n]
        if mask is not None:
            log_w = jnp.where(mask, log_w, NEG)
        pv.append(_dot(jnp.exp(log_w).astype(BF16), v_ref[bb, pl.ds(start, blk), c]))
    amax = None
    for n, (bb, c) in enumerate(chains):
        across = both[n][:, blk:]
        if first:
            acc_scr[bb, :, c] = pv[n]
        else:
            acc_scr[bb, :, c] += pv[n]
            across = across + prev[n]
        across_scr[n] = across
        amax = across if amax is None else jnp.maximum(amax, across)
    return (jnp.max(amax) > BF16_EXP_UNDERFLOW).astype(jnp.int32)


def _sb_kernel(q_ref, k_ref, v_ref, out_ref, cmat_ref, across_scr, acc_scr):
    blk = SB_BLOCK
    i = pl.program_id(1)

    @pl.when(jnp.logical_and(pl.program_id(0) == 0, i == 0))
    def _():
        ri = lax.broadcasted_iota(jnp.int32, (2 * blk, 2 * blk), 0)
        ci = lax.broadcasted_iota(jnp.int32, (2 * blk, 2 * blk), 1)
        cmat_ref[...] = jnp.logical_or(ci >= blk, (ri % blk) >= ci).astype(BF16)

    block = functools.partial(_sb_block, q_ref, k_ref, v_ref, cmat_ref, across_scr, acc_scr, i)
    go = block(i, causal=True, validity=True, first=True)

    def cond(carry):
        j, go = carry
        return jnp.logical_and(j >= 1, go > 0)

    def body(carry):
        j, _ = carry
        return j - 1, block(j, causal=False, validity=False, first=False)

    j, go = lax.while_loop(cond, body, (i - 1, go))

    @pl.when(jnp.logical_and(j == 0, go > 0))
    def _():
        block(0, causal=False, validity=True, first=False)

    out_ref[...] = acc_scr[...].astype(BF16)


def _batch_rows(b, most=2):
    rows = most
    while b % rows:
        rows //= 2
    return rows


def _stick_breaking(qs, ks, vs):
    b, t, _ = qs.shape
    blk = SB_BLOCK
    nb = _batch_rows(b)
    scr = (_nbytes((2 * blk, 2 * blk), BF16) + _nbytes((nb * SB_HEADS, blk, blk), F32)
           + _nbytes((nb, blk, SBW), F32))
    vmem = 4 * _nbytes((nb, blk, SBW), BF16) + 2 * _nbytes((nb, t, SBW), BF16) + scr + (8 << 20)
    resident = dict(pipeline_mode=pl.Buffered(1))
    return pl.pallas_call(
        _sb_kernel,
        grid=(b // nb, t // blk),
        in_specs=[pl.BlockSpec((nb, blk, SBW), lambda bi, qi: (bi, qi, 0)),
                  pl.BlockSpec((nb, t, SBW), lambda bi, qi: (bi, 0, 0), **resident),
                  pl.BlockSpec((nb, t, SBW), lambda bi, qi: (bi, 0, 0), **resident)],
        out_specs=pl.BlockSpec((nb, blk, SBW), lambda bi, qi: (bi, qi, 0)),
        out_shape=jax.ShapeDtypeStruct((b, t, SBW), BF16),
        scratch_shapes=[pltpu.VMEM((2 * blk, 2 * blk), BF16),
                        pltpu.VMEM((nb * SB_HEADS, blk, blk), F32),
                        pltpu.VMEM((nb, blk, SBW), F32)],
        compiler_params=pltpu.CompilerParams(
            dimension_semantics=("arbitrary", "arbitrary"),
            vmem_limit_bytes=int(vmem)),
        name="stick_breaking",
    )(qs, ks, vs)


def _out_proj_kernel(h_ref, hm_ref, hs_ref, wm_ref, ws_ref, g_ref, out_ref):
    y = _dot(hm_ref[...], wm_ref[...]) + _dot(hs_ref[...], ws_ref[...])
    out_ref[...] = h_ref[...] + _rms(y, g_ref[...])


def _out_proj(h2, hm2, hs2, w_out, g1):
    n, d = h2.shape
    tm = _seq_tile(n)
    wb = w_out.astype(BF16)
    wm, ws = wb[:MV], wb[MV:]
    blk = (2 * _nbytes((tm, d), F32) + _nbytes((tm, MV), BF16) + _nbytes((tm, SBW), BF16)
           + _nbytes(wb.shape, BF16) + _nbytes((1, d), F32))
    return pl.pallas_call(
        _out_proj_kernel,
        grid=(n // tm,),
        in_specs=[pl.BlockSpec((tm, d), lambda i: (i, 0)),
                  pl.BlockSpec((tm, MV), lambda i: (i, 0)),
                  pl.BlockSpec((tm, SBW), lambda i: (i, 0)),
                  pl.BlockSpec(wm.shape, lambda i: (0, 0)),
                  pl.BlockSpec(ws.shape, lambda i: (0, 0)),
                  pl.BlockSpec((1, d), lambda i: (0, 0))],
        out_specs=pl.BlockSpec((tm, d), lambda i: (i, 0)),
        out_shape=jax.ShapeDtypeStruct((n, d), F32),
        compiler_params=pltpu.CompilerParams(
            dimension_semantics=("arbitrary",),
            vmem_limit_bytes=_vmem_limit(blk, 0, 2 * _nbytes((tm, d), F32))),
        name="mixer_out_proj",
    )(h2, hm2, hs2, wm, ws, g1.reshape(1, d))


def _conformer_kernel(h_ref, g0_ref, w1_ref, b1_ref, wdw_ref, bdw_ref, lng_ref, lnb_ref,
                      w2_ref, b2_ref, g1_ref, out_ref, conv_scr):
    t = pl.program_id(1)
    tm, d = h_ref.shape[1], h_ref.shape[2]
    ts = tm // CONV_SUBTILES
    subs = range(CONV_SUBTILES)

    @pl.when(t == 0)
    def _():
        conv_scr[0:CONV_HALO, :] = jnp.zeros((CONV_HALO, d), F32)

    for s in subs:
        ub = _rms(h_ref[0, s * ts:(s + 1) * ts], g0_ref[...]).astype(BF16)
        ag = _dot(ub, w1_ref[...]) + b1_ref[...]
        row = t * tm + s * ts + lax.broadcasted_iota(jnp.int32, (ts, 1), 0)
        y = jnp.where(row >= PAD_FRONT, ag[:, :d] * jax.nn.sigmoid(ag[:, d:]), 0.0)
        conv_scr[CONV_HALO + s * ts:CONV_HALO + (s + 1) * ts, :] = y

    conv = []
    for s in subs:
        acc = None
        for r in range(SUBLANES):
            part = None
            for a in range(CONV_HALO // SUBLANES):
                lag = SUBLANES * a + r
                if lag >= CONV_WIDTH:
                    continue
                j = CONV_WIDTH - 1 - lag
                start = CONV_HALO + s * ts - SUBLANES * (a + 1)
                term = wdw_ref[j:j + 1, :] * conv_scr[start:start + ts + SUBLANES, :]
                part = term if part is None else part + term
            if r:
                part = pltpu.roll(part, r, axis=0)
            acc = part if acc is None else acc + part
        conv.append(acc[SUBLANES:SUBLANES + ts] + bdw_ref[...])

    for s in subs:
        mu = jnp.mean(conv[s], axis=-1, keepdims=True)
        cen = conv[s] - mu
        var = jnp.mean(cen * cen, axis=-1, keepdims=True)
        ln = cen * lax.rsqrt(var + EPS) * lng_ref[...] + lnb_ref[...]
        act = (ln * jax.nn.sigmoid(ln)).astype(BF16)
        z = _dot(act, w2_ref[...]) + b2_ref[...]
        out_ref[0, s * ts:(s + 1) * ts] = h_ref[0, s * ts:(s + 1) * ts] + _rms(z, g1_ref[...])

    conv_scr[0:CONV_HALO, :] = conv_scr[tm:tm + CONV_HALO, :]


def _conformer(h, g0, g1, w_pw1, b_pw1, w_dw, b_dw, ln_g, ln_b, w_pw2, b_pw2):
    b, t, d = h.shape
    tm = _seq_tile(t)
    ins = [h, g0.reshape(1, d), w_pw1.astype(BF16), b_pw1.reshape(1, -1), w_dw, b_dw.reshape(1, d),
           ln_g.reshape(1, d), ln_b.reshape(1, d), w_pw2.astype(BF16), b_pw2.reshape(1, d), g1.reshape(1, d)]

    def full(a):
        return pl.BlockSpec(a.shape, lambda i, j: (0,) * a.ndim)

    blk = 2 * _nbytes((tm, d), F32) + sum(_nbytes(a.shape, a.dtype) for a in ins[1:])
    assert tm % (CONV_SUBTILES * CONV_HALO) == 0
    scr = _nbytes((tm + CONV_HALO, d), F32)
    return pl.pallas_call(
        _conformer_kernel,
        grid=(b, t // tm),
        in_specs=[pl.BlockSpec((1, tm, d), lambda i, j: (i, j, 0))] + [full(a) for a in ins[1:]],
        out_specs=pl.BlockSpec((1, tm, d), lambda i, j: (i, j, 0)),
        out_shape=jax.ShapeDtypeStruct((b, t, d), F32),
        scratch_shapes=[pltpu.VMEM((tm + CONV_HALO, d), F32)],
        compiler_params=pltpu.CompilerParams(
            dimension_semantics=("arbitrary", "arbitrary"),
            vmem_limit_bytes=_vmem_limit(blk, scr, 6 * _nbytes((tm, d), F32))),
        name="conformer_conv",
    )(*ins)


def _ffn_kernel(h_ref, res_ref, g2_ref, wg_ref, wu_ref, wd_ref, g3_ref, out_ref,
                u_even, u_odd, acc_even, acc_odd, *, chunks):
    i = pl.program_id(0)

    @pl.when(i == 0)
    def _():
        for ref in (u_even, u_odd, acc_even, acc_odd):
            ref[...] = jnp.zeros(ref.shape, ref.dtype)

    def step(u_new, u_prev, acc_new, acc_prev):
        out_ref[...] = res_ref[...] + _rms(acc_prev[...], g3_ref[...])
        u_new[...] = _rms(h_ref[...], g2_ref[...]).astype(BF16)
        u = u_prev[...]
        acc = None
        for lo, hi in chunks:
            a = _dot(u, wg_ref[:, lo:hi])
            hid = (a * jax.nn.sigmoid(a) * _dot(u, wu_ref[:, lo:hi])).astype(BF16)
            part = _dot(hid, wd_ref[lo:hi, :])
            acc = part if acc is None else acc + part
        acc_new[...] = acc

    @pl.when(i % 2 == 0)
    def _():
        step(u_even, u_odd, acc_odd, acc_even)

    @pl.when(i % 2 == 1)
    def _():
        step(u_odd, u_even, acc_even, acc_odd)


FFN_PIPELINE_LAG = 2
MXU_DIM = 256


def _ffn_chunks(hidden):
    if hidden % MXU_DIM:
        return ((0, hidden),)
    units = hidden // MXU_DIM
    pieces = -(-units // 6)
    bounds = [round(units * p / pieces) * MXU_DIM for p in range(pieces + 1)]
    return tuple(zip(bounds[:-1], bounds[1:]))


def _ffn(h, g2, g3, w_gate, w_up, w_down, skip=0):
    b, t, d = h.shape
    hidden = w_gate.shape[1]
    chunks = _ffn_chunks(hidden)
    weights = (g2.reshape(1, d), w_gate.astype(BF16), w_up.astype(BF16), w_down.astype(BF16), g3.reshape(1, d))
    lag = FFN_PIPELINE_LAG
    if skip == 0:
        rows = b * t
        tm = _seq_tile(rows, cap=512)
        n_tiles = rows // tm
        operand = h.reshape(rows, d)
        in_tile = lambda i: (jnp.minimum(i, n_tiles - 1), 0)
        out_tile = lambda i: (jnp.maximum(i - lag, 0), 0)
        h_spec = pl.BlockSpec((tm, d), in_tile)
        res_spec = pl.BlockSpec((tm, d), out_tile)
        out_spec = pl.BlockSpec((tm, d), out_tile)
        out_shape = jax.ShapeDtypeStruct((rows, d), F32)
    else:
        tm = _seq_tile(t - skip, cap=512)
        per_row = (t - skip) // tm
        n_tiles = b * per_row
        operand = h
        assert skip % SUBLANES == 0

        def rows_at(tile):
            return (tile // per_row, pl.multiple_of(skip + (tile % per_row) * tm, SUBLANES), 0)

        def out_tile(i):
            tile = jnp.maximum(i - lag, 0)
            return (tile // per_row, tile % per_row, 0)

        elem = (None, pl.Element(tm), pl.Element(d))
        h_spec = pl.BlockSpec(elem, lambda i: rows_at(jnp.minimum(i, n_tiles - 1)))
        res_spec = pl.BlockSpec(elem, lambda i: rows_at(jnp.maximum(i - lag, 0)))
        out_spec = pl.BlockSpec((None, tm, d), out_tile)
        out_shape = jax.ShapeDtypeStruct((b, t - skip, d), F32)

    def resident(a):
        return pl.BlockSpec(a.shape, lambda i: (0,) * a.ndim, pipeline_mode=pl.Buffered(1))

    widest = max(hi - lo for lo, hi in chunks)
    vmem = (6 * _nbytes((tm, d), F32) + sum(_nbytes(a.shape, a.dtype) for a in weights)
            + 2 * _nbytes((tm, d), BF16) + 2 * _nbytes((tm, d), F32)
            + len(chunks) * 3 * _nbytes((tm, widest), F32) + 2 * _nbytes((tm, d), F32) + (4 << 20))
    out = pl.pallas_call(
        functools.partial(_ffn_kernel, chunks=chunks),
        grid=(n_tiles + lag,),
        in_specs=[h_spec, res_spec] + [resident(a) for a in weights],
        out_specs=out_spec,
        out_shape=out_shape,
        scratch_shapes=[pltpu.VMEM((tm, d), BF16), pltpu.VMEM((tm, d), BF16),
                        pltpu.VMEM((tm, d), F32), pltpu.VMEM((tm, d), F32)],
        compiler_params=pltpu.CompilerParams(
            dimension_semantics=("arbitrary",),
            vmem_limit_bytes=int(min(vmem, V7X_VMEM_BYTES - (6 << 20)))),
        name="swiglu_ffn",
    )(operand, operand, *weights)
    return out if skip else out.reshape(b, t, d)


def kernel(x, meta, norm_g, mix_w_in, mix_qk_conv_w, mix_qk_conv_b, mix_gate_b, mix_hnorm_g, mix_w_out,
           conv_w_pw1, conv_b_pw1, conv_w_dw, conv_b_dw, conv_ln_g, conv_ln_b, conv_w_pw2, conv_b_pw2,
           ffn_w_gate, ffn_w_up, ffn_w_down):
    b, seq, d = x.shape
    depth = norm_g.shape[0]
    t = seq + N_META + PAD_FRONT
    h = None
    if _embed_tile(seq) is None:
        h = jnp.concatenate([jnp.zeros((b, PAD_FRONT, d), x.dtype),
                             jnp.broadcast_to(meta[None].astype(x.dtype), (b, N_META, d)), x], axis=1)
    for layer in range(depth):
        g = norm_g[layer]
        i = layer // 2
        if layer % 2 == 0:
            if h is None:
                qm, km, vm, om, gt, qs, ks, vs, h = _in_proj(x, g[0], mix_w_in[i], mix_qk_conv_w[i],
                                                             mix_qk_conv_b[i], mix_gate_b[i], meta=meta)
            else:
                qm, km, vm, om, gt, qs, ks, vs = _in_proj(h, g[0], mix_w_in[i], mix_qk_conv_w[i],
                                                          mix_qk_conv_b[i], mix_gate_b[i])
            hm = _mlstm(qm, km, vm, om, gt, mix_hnorm_g[i])
            hs = _stick_breaking(qs, ks, vs)
            h1 = _out_proj(h.reshape(b * t, d), hm.reshape(b * t, MV), hs.reshape(b * t, SBW),
                           mix_w_out[i], g[1]).reshape(b, t, d)
        else:
            h1 = _conformer(h, g[0], g[1], conv_w_pw1[i], conv_b_pw1[i], conv_w_dw[i], conv_b_dw[i],
                            conv_ln_g[i], conv_ln_b[i], conv_w_pw2[i], conv_b_pw2[i])
        skip = N_META + PAD_FRONT if layer == depth - 1 else 0
        h = _ffn(h1, g[2], g[3], ffn_w_gate[layer], ffn_w_up[layer], ffn_w_down[layer], skip=skip)
    return h
```
